```python
import jax
import jax.numpy as jnp
from jax import lax
import numpy as np

D_MODEL = 1024
BATCH = 8
SEQ = 2048
DEPTH = 2

CTX_LEN = 256
GRID_W = 64

NA_HEADS = 4
NA_HEAD_DIM = 64
NA_WIN_R = 8
NA_WIN_C = 16

WA_HEADS = 4
WA_KV_HEADS = 2
WA_HEAD_DIM = 64
WA_WINDOW = 128
WA_BLOCK = 128

GLA_HEADS = 4
GLA_DK = 64
GLA_DV = 128
GLA_GATE_RANK = 16
GLA_GATE_TAU = 16.0
GLA_CHUNK = 64

NA_WIDTH = NA_HEADS * NA_HEAD_DIM
WA_Q_WIDTH = WA_HEADS * WA_HEAD_DIM
WA_KV_WIDTH = WA_KV_HEADS * WA_HEAD_DIM
GLA_QK_WIDTH = GLA_HEADS * GLA_DK
GLA_V_WIDTH = GLA_HEADS * GLA_DV
MIX_WIDTH = NA_WIDTH + WA_Q_WIDTH + GLA_V_WIDTH
IN_WIDTH = 3 * NA_WIDTH + WA_Q_WIDTH + 2 * WA_KV_WIDTH + 2 * GLA_QK_WIDTH + 2 * GLA_V_WIDTH + 2 * GLA_GATE_RANK

FF_DIM = 2816
FF_CONV = 3

ROPE_THETA = 10000.0
EPS = 1e-6

kernel_name = 'hybrid_na_swa_gla_dit_trunk'


def rms_norm(x, g):
    xf = x.astype(jnp.float32)
    y = xf * lax.rsqrt(jnp.mean(xf * xf, axis=-1, keepdims=True) + EPS)
    return (y * g.astype(jnp.float32)).astype(x.dtype)


def modulate(x, g, shift, scale):
    return rms_norm(x, g) * (1 + scale) + shift


def split_heads(a, n_heads):
    return a.reshape(a.shape[0], a.shape[1], n_heads, a.shape[2] // n_heads)


def in_split_points():
    widths = (NA_WIDTH, NA_WIDTH, NA_WIDTH, WA_Q_WIDTH, WA_KV_WIDTH, WA_KV_WIDTH,
              GLA_QK_WIDTH, GLA_QK_WIDTH, GLA_V_WIDTH, GLA_V_WIDTH, GLA_GATE_RANK)
    points, acc = [], 0
    for w in widths:
        acc += w
        points.append(acc)
    return points


def mixer_inputs(h, w_in, gate_w, gate_b):
    (na_q, na_k, na_v, wa_q, wa_k, wa_v, g_q, g_k, g_v, g_out, lr_f, lr_b) = jnp.split(
        h @ w_in, in_split_points(), axis=-1)

    def log_decay(lr, d):
        logit = (lr @ gate_w[d] + gate_b[d]).astype(jnp.float32)
        return split_heads(jax.nn.log_sigmoid(logit) / GLA_GATE_TAU, GLA_HEADS)

    return dict(
        na_q=split_heads(na_q, NA_HEADS), na_k=split_heads(na_k, NA_HEADS), na_v=split_heads(na_v, NA_HEADS),
        wa_q=split_heads(wa_q, WA_HEADS), wa_k=split_heads(wa_k, WA_KV_HEADS), wa_v=split_heads(wa_v, WA_KV_HEADS),
        g_q=split_heads(g_q, GLA_HEADS) * GLA_DK ** -0.5, g_k=split_heads(g_k, GLA_HEADS),
        g_v=split_heads(g_v, GLA_HEADS), g_out=g_out,
        la_f=log_decay(lr_f, 0), la_b=log_decay(lr_b, 1))


def axial_rope(n_tokens, head_dim):
    t = jnp.arange(n_tokens)
    row = (t // GRID_W).astype(jnp.float32)
    col = (t % GRID_W).astype(jnp.float32)
    n_freq = head_dim // 4
    inv = ROPE_THETA ** (-jnp.arange(n_freq, dtype=jnp.float32) / n_freq)
    ang = jnp.concatenate([row[:, None] * inv, col[:, None] * inv], axis=-1)
    return jnp.cos(ang), jnp.sin(ang)


def apply_rope(x, cos, sin):
    x1, x2 = jnp.split(x.astype(jnp.float32), 2, axis=-1)
    c, s = cos[:, None, :], sin[:, None, :]
    return jnp.concatenate([x1 * c - x2 * s, x1 * s + x2 * c], axis=-1).astype(x.dtype)


def context_attention(q, k, v, sink=None):
    B, L, H, Dh = q.shape
    Lk, Hkv = k.shape[1], k.shape[2]
    G = H // Hkv
    qg = q.reshape(B, L, Hkv, G, Dh)
    s = jnp.einsum('bqkgd,blkd->bkgql', qg, k).astype(jnp.float32) * Dh ** -0.5
    if sink is not None:
        s_sink = jnp.broadcast_to(sink.astype(jnp.float32).reshape(1, Hkv, G, 1, 1), s.shape[:-1] + (1,))
        s = jnp.concatenate([s, s_sink], axis=-1)
    p = jax.nn.softmax(s, axis=-1)[..., :Lk].astype(v.dtype)
    return jnp.einsum('bkgql,blkd->bqkgd', p, v).reshape(B, L, H, Dh)


def neighbourhood_attention(q, k, v, k_ctx, v_ctx, rpb):
    B, S, H, Dh = q.shape
    rows = S // GRID_W
    win_r = min(NA_WIN_R, rows)
    scale = Dh ** -0.5
    qg = q.reshape(B, rows, GRID_W, H, Dh)
    kg = k.reshape(B, rows, GRID_W, H, Dh)
    vg = v.reshape(B, rows, GRID_W, H, Dh)
    col = jnp.arange(GRID_W)
    col_start = jnp.clip(col - NA_WIN_C // 2, 0, GRID_W - NA_WIN_C)
    col_idx = col_start[:, None] + jnp.arange(NA_WIN_C)[None, :]
    col_off = col_idx - col[:, None] + (NA_WIN_C - 1)
    n_nb = win_r * NA_WIN_C

    def one_row(r):
        rs = jnp.clip(r - win_r // 2, 0, rows - win_r)
        kn = lax.dynamic_slice_in_dim(kg, rs, win_r, axis=1)[:, :, col_idx]
        vn = lax.dynamic_slice_in_dim(vg, rs, win_r, axis=1)[:, :, col_idx]
        qr = lax.dynamic_index_in_dim(qg, r, axis=1, keepdims=False)
        row_off = rs + jnp.arange(win_r) - r + (NA_WIN_R - 1)
        bias = rpb[:, row_off][:, :, col_off].transpose(0, 2, 1, 3).astype(jnp.float32)
        s_nb = jnp.einsum('bqhd,brqjhd->bhqrj', qr, kn).astype(jnp.float32) * scale + bias[None]
        s_ctx = jnp.einsum('bqhd,blhd->bhql', qr, k_ctx).astype(jnp.float32) * scale
        s = jnp.concatenate([s_nb.reshape(B, H, GRID_W, n_nb), s_ctx], axis=-1)
        p = jax.nn.softmax(s, axis=-1).astype(v.dtype)
        p_nb = p[..., :n_nb].reshape(B, H, GRID_W, win_r, NA_WIN_C)
        return (jnp.einsum('bhqrj,brqjhd->bqhd', p_nb, vn)
                + jnp.einsum('bhql,blhd->bqhd', p[..., n_nb:], v_ctx))

    out = lax.map(one_row, jnp.arange(rows))
    return out.transpose(1, 0, 2, 3, 4).reshape(B, S, H, Dh)


def window_attention(q, k, v, k_ctx, v_ctx, sink):
    B, S, H, Dh = q.shape
    Hkv = k.shape[2]
    G = H // Hkv
    nb = S // WA_BLOCK
    n_loc = 3 * WA_BLOCK
    L = k_ctx.shape[1]
    scale = Dh ** -0.5

    def band(a):
        ap = jnp.pad(a, ((0, 0), (WA_BLOCK, WA_BLOCK), (0, 0), (0, 0))).reshape(B, nb + 2, WA_BLOCK, Hkv, Dh)
        return jnp.concatenate([ap[:, :-2], ap[:, 1:-1], ap[:, 2:]], axis=2)

    kb, vb = band(k), band(v)
    qb = q.reshape(B, nb, WA_BLOCK, Hkv, G, Dh)
    s_loc = jnp.einsum('bnqkgd,bnskd->bnkgqs', qb, kb).astype(jnp.float32) * scale
    q_pos = jnp.arange(nb)[:, None] * WA_BLOCK + jnp.arange(WA_BLOCK)[None, :]
    k_pos = jnp.arange(nb)[:, None] * WA_BLOCK - WA_BLOCK + jnp.arange(n_loc)[None, :]
    valid = ((jnp.abs(q_pos[:, :, None] - k_pos[:, None, :]) <= WA_WINDOW)
             & (k_pos[:, None, :] >= 0) & (k_pos[:, None, :] < S))
    s_loc = jnp.where(valid[None, :, None, None], s_loc, -jnp.inf)
    s_ctx = jnp.einsum('bnqkgd,blkd->bnkgql', qb, k_ctx).astype(jnp.float32) * scale
    s_sink = jnp.broadcast_to(sink.astype(jnp.float32).reshape(1, 1, Hkv, G, 1, 1), s_loc.shape[:-1] + (1,))
    p = jax.nn.softmax(jnp.concatenate([s_loc, s_ctx, s_sink], axis=-1), axis=-1).astype(v.dtype)
    o = (jnp.einsum('bnkgqs,bnskd->bnqkgd', p[..., :n_loc], vb)
         + jnp.einsum('bnkgql,blkd->bnqkgd', p[..., n_loc:n_loc + L], v_ctx))
    return o.reshape(B, S, H, Dh)


def gla_chunk_scan(q, k, v, log_a, state0):
    B, T, H, Dk = q.shape
    Dv = v.shape[-1]
    n = T // GLA_CHUNK

    def to_chunks(a):
        return a.astype(jnp.float32).reshape(B, n, GLA_CHUNK, H, a.shape[-1]).transpose(1, 0, 3, 2, 4)

    lower_tri = jnp.tril(jnp.ones((GLA_CHUNK, GLA_CHUNK), dtype=bool))[:, :, None]

    def step(state, inp):
        qc, kc, vc, gc = inp
        b = jnp.cumsum(gc, axis=2)
        b_last = b[:, :, -1:, :]
        o_inter = jnp.einsum('bhck,bhkv->bhcv', qc * jnp.exp(b), state)
        rel = jnp.exp(jnp.where(lower_tri, b[:, :, :, None, :] - b[:, :, None, :, :], -jnp.inf))
        att = jnp.einsum('bhtk,bhsk,bhtsk->bhts', qc, kc, rel)
        o_intra = jnp.einsum('bhts,bhsv->bhtv', att, vc)
        new_state = (jnp.exp(b_last[:, :, 0, :])[..., None] * state
                     + jnp.einsum('bhsk,bhsv->bhkv', kc * jnp.exp(b_last - b), vc))
        return new_state, o_inter + o_intra

    state, o = lax.scan(step, state0, (to_chunks(q), to_chunks(k), to_chunks(v), to_chunks(log_a)))
    o = o.transpose(1, 0, 3, 2, 4).reshape(B, T, H, Dv)
    return o.astype(v.dtype), state


def gla_final_state(k, v, log_a):
    b = jnp.cumsum(log_a.astype(jnp.float32), axis=1)
    w = jnp.exp(b[:, -1:] - b)
    return jnp.einsum('bthk,bthv->bhkv', k.astype(jnp.float32) * w, v.astype(jnp.float32))


def bidirectional_gla(q, k, v, la_f, la_b, q_c, k_c, v_c, la_f_c, la_b_c, need_ctx_out):
    B = k_c.shape[0]
    zero = jnp.zeros((B, GLA_HEADS, GLA_DK, GLA_DV), jnp.float32)
    rev = lambda a: jnp.flip(a, axis=1)
    if need_ctx_out:
        oc_f, s_f = gla_chunk_scan(q_c, k_c, v_c, la_f_c, zero)
        oc_b, s_b = gla_chunk_scan(rev(q_c), rev(k_c), rev(v_c), rev(la_b_c), zero)
        o_ctx = oc_f + rev(oc_b)
    else:
        s_f = gla_final_state(k_c, v_c, la_f_c)
        s_b = gla_final_state(rev(k_c), rev(v_c), rev(la_b_c))
        o_ctx = None
    o_f, _ = gla_chunk_scan(q, k, v, la_f, s_f)
    o_b, _ = gla_chunk_scan(rev(q), rev(k), rev(v), rev(la_b), s_b)
    return o_f + rev(o_b), o_ctx


def merge_groups(o_na, o_wa, o_gla, g_out, gla_norm_g, w_out):
    B, T = o_na.shape[:2]
    o_gla = rms_norm(o_gla, gla_norm_g).reshape(B, T, GLA_V_WIDTH) * jax.nn.silu(g_out)
    y = jnp.concatenate([o_na.reshape(B, T, NA_WIDTH), o_wa.reshape(B, T, WA_Q_WIDTH), o_gla], axis=-1)
    return y @ w_out


def conv_ffn(h, w_up, conv_w, conv_b, w_down):
    value, gate = jnp.split(h @ w_up, 2, axis=-1)
    gate = lax.conv_general_dilated(
        gate, conv_w[:, None, :].astype(gate.dtype), window_strides=(1,),
        padding=((FF_CONV // 2, FF_CONV // 2),), dimension_numbers=('NWC', 'WIO', 'NWC'),
        feature_group_count=FF_DIM) + conv_b
    return (jax.nn.gelu(gate, approximate=False) * value) @ w_down


def setup_inputs(seed: int = 0) -> dict:
    key = jax.random.key(seed)
    ks = jax.random.split(key, 20)

    def nrm(k, shape, scale):
        return jax.random.normal(k, shape, jnp.float32) * scale

    return {
        'x': nrm(ks[0], (BATCH, SEQ, D_MODEL), 1.0),
        'c': nrm(ks[1], (BATCH, D_MODEL), 1.0),
        'ctx': nrm(ks[2], (BATCH, CTX_LEN, D_MODEL), 1.0),
        'c_ctx': nrm(ks[3], (D_MODEL,), 1.0),
        'w_mod': nrm(ks[4], (DEPTH, D_MODEL, 6 * D_MODEL), 0.5 * D_MODEL ** -0.5),
        'b_mod': nrm(ks[5], (DEPTH, 6 * D_MODEL), 0.02),
        'norm1_g': 1.0 + nrm(ks[6], (DEPTH, D_MODEL), 0.05),
        'norm2_g': 1.0 + nrm(ks[7], (DEPTH, D_MODEL), 0.05),
        'w_in': nrm(ks[8], (DEPTH, D_MODEL, IN_WIDTH), D_MODEL ** -0.5),
        'na_rpb': nrm(ks[9], (DEPTH, NA_HEADS, 2 * NA_WIN_R - 1, 2 * NA_WIN_C - 1), 0.1),
        'wa_sink': nrm(ks[10], (DEPTH, WA_HEADS), 0.5),
        'gla_gate_w': nrm(ks[11], (DEPTH, 2, GLA_GATE_RANK, GLA_QK_WIDTH), GLA_GATE_RANK ** -0.5),
        'gla_gate_b': nrm(ks[12], (DEPTH, 2, GLA_QK_WIDTH), 0.1),
        'gla_norm_g': 1.0 + nrm(ks[13], (DEPTH, GLA_DV), 0.05),
        'w_out': nrm(ks[14], (DEPTH, MIX_WIDTH, D_MODEL), MIX_WIDTH ** -0.5),
        'ffn_w_up': nrm(ks[15], (DEPTH, D_MODEL, 2 * FF_DIM), D_MODEL ** -0.5),
        'ffn_conv_w': nrm(ks[16], (DEPTH, FF_CONV, FF_DIM), FF_CONV ** -0.5),
        'ffn_conv_b': nrm(ks[17], (DEPTH, FF_DIM), 0.02),
        'ffn_w_down': nrm(ks[18], (DEPTH, FF_DIM, D_MODEL), FF_DIM ** -0.5),
        'final_norm_g': 1.0 + nrm(ks[19], (D_MODEL,), 0.05),
    }


def reference(x, c, ctx, c_ctx, w_mod, b_mod, norm1_g, norm2_g, w_in, na_rpb, wa_sink,
              gla_gate_w, gla_gate_b, gla_norm_g, w_out, ffn_w_up, ffn_conv_w, ffn_conv_b,
              ffn_w_down, final_norm_g):
    S = x.shape[1]
    cos, sin = axial_rope(S, WA_HEAD_DIM)
    xc = ctx
    silu_c = jax.nn.silu(c)
    silu_cc = jax.nn.silu(c_ctx)
    for i in range(DEPTH):
        last = i == DEPTH - 1
        sh1, sc1, gt1, sh2, sc2, gt2 = jnp.split((silu_c @ w_mod[i] + b_mod[i])[:, None, :], 6, axis=-1)
        csh1, csc1, cgt1, csh2, csc2, cgt2 = jnp.split(silu_cc @ w_mod[i] + b_mod[i], 6, axis=-1)

        lat = mixer_inputs(modulate(x, norm1_g[i], sh1, sc1), w_in[i], gla_gate_w[i], gla_gate_b[i])
        cx = mixer_inputs(modulate(xc, norm1_g[i], csh1, csc1), w_in[i], gla_gate_w[i], gla_gate_b[i])

        o_na = neighbourhood_attention(lat['na_q'], lat['na_k'], lat['na_v'], cx['na_k'], cx['na_v'], na_rpb[i])
        o_wa = window_attention(apply_rope(lat['wa_q'], cos, sin), apply_rope(lat['wa_k'], cos, sin),
                                lat['wa_v'], cx['wa_k'], cx['wa_v'], wa_sink[i])
        o_gla, oc_gla = bidirectional_gla(lat['g_q'], lat['g_k'], lat['g_v'], lat['la_f'], lat['la_b'],
                                          cx['g_q'], cx['g_k'], cx['g_v'], cx['la_f'], cx['la_b'],
                                          not last)
        x = x + gt1 * merge_groups(o_na, o_wa, o_gla, lat['g_out'], gla_norm_g[i], w_out[i])

        x = x + gt2 * conv_ffn(modulate(x, norm2_g[i], sh2, sc2), ffn_w_up[i], ffn_conv_w[i], ffn_conv_b[i], ffn_w_down[i])

        if not last:
            oc_na = context_attention(cx['na_q'], cx['na_k'], cx['na_v'])
            oc_wa = context_attention(cx['wa_q'], cx['wa_k'], cx['wa_v'], wa_sink[i])
            xc = xc + cgt1 * merge_groups(oc_na, oc_wa, oc_gla, cx['g_out'], gla_norm_g[i], w_out[i])
            xc = xc + cgt2 * conv_ffn(modulate(xc, norm2_g[i], csh2, csc2), ffn_w_up[i], ffn_conv_w[i],
                                      ffn_conv_b[i], ffn_w_down[i])
    return rms_norm(x, final_norm_g)
```

```python
import functools
import math

import numpy as np
import jax
import jax.numpy as jnp
from jax import lax
from jax.experimental import pallas as pl
from jax.experimental.pallas import tpu as pltpu

D_MODEL = 1024
SEQ = 2048
DEPTH = 2
CTX_LEN = 256
GRID_W = 64
GRID_ROWS = SEQ // GRID_W
TOK = CTX_LEN + SEQ

NA_HEADS = 4
NA_HEAD_DIM = 64
NA_WIN_R = 8
NA_WIN_C = 16
WA_HEADS = 4
WA_KV_HEADS = 2
WA_HEAD_DIM = 64
WA_WINDOW = 128
WA_BLOCK = 128
GLA_HEADS = 4
GLA_DK = 64
GLA_DV = 128
GLA_GATE_RANK = 16
GLA_GATE_TAU = 16.0
GLA_CHUNK = 64

NA_WIDTH = NA_HEADS * NA_HEAD_DIM
WA_Q_WIDTH = WA_HEADS * WA_HEAD_DIM
WA_KV_WIDTH = WA_KV_HEADS * WA_HEAD_DIM
GLA_QK_WIDTH = GLA_HEADS * GLA_DK
GLA_V_WIDTH = GLA_HEADS * GLA_DV
MIX_WIDTH = NA_WIDTH + WA_Q_WIDTH + GLA_V_WIDTH

FF_DIM = 2816
FF_CHUNK = 1408
ROPE_THETA = 10000.0
EPS = 1e-6

TILE = 256
N_TILES = TOK // TILE
LANES = 128
SUBLANES = 8
NEG = -1e30
VMEM_LIMIT = 56 * 1024 * 1024

F32 = jnp.float32
BF16 = jnp.bfloat16
NT_DIMS = (((1,), (1,)), ((), ()))
TN_DIMS = (((0,), (0,)), ((), ()))

C_NA = 0
C_WAQ = 768
C_WAQS = 1024
C_WAK = 1280
C_WAKS = 1408
C_WAV = 1536
C_GQ = 1664
C_GK = 1920
C_GV = 2176
C_GO = 2688
C_LR = 3200
IN_COLS = 3328

WA_HEAD_ORDER = (0, 2, 1, 3)


def _cparams(sem):
    return pltpu.CompilerParams(dimension_semantics=sem, vmem_limit_bytes=VMEM_LIMIT)


def _lane_lo(shape, width=LANES):
    lane = lax.broadcasted_iota(jnp.int32, shape, len(shape) - 1)
    return (lane & (width - 1)) < (width // 2)


def _mod_kernel(c_ref, w_ref, b_ref, o_ref):
    c = c_ref[...]
    sc = c * jax.nn.sigmoid(c)
    o_ref[...] = jnp.dot(sc.astype(BF16), w_ref[...].astype(BF16),
                         preferred_element_type=F32) + b_ref[...]


def _mod_call(cc, w_mod, b_mod):
    tn = 1536
    rows = cc.shape[0]
    return pl.pallas_call(
        _mod_kernel,
        grid=(DEPTH, 6 * D_MODEL // tn),
        in_specs=[
            pl.BlockSpec((rows, D_MODEL), lambda l, j: (0, 0)),
            pl.BlockSpec((None, D_MODEL, tn), lambda l, j: (l, 0, j)),
            pl.BlockSpec((None, 1, tn), lambda l, j: (l, 0, j)),
        ],
        out_specs=pl.BlockSpec((None, rows, tn), lambda l, j: (l, 0, j)),
        out_shape=jax.ShapeDtypeStruct((DEPTH, rows, 6 * D_MODEL), F32),
        compiler_params=_cparams(("arbitrary", "arbitrary")),
        name="adaln_mod",
    )(cc, w_mod, b_mod.reshape(DEPTH, 1, 6 * D_MODEL))


N_RO = 2 * NA_WIN_R - 1
N_CO = 2 * NA_WIN_C - 1


def _nabias_kernel(rpb_ref, o_ref):
    h = pl.program_id(0)
    shape = (GRID_W, LANES)
    q = lax.broadcasted_iota(jnp.int32, shape, 0)
    lane = lax.broadcasted_iota(jnp.int32, shape, 1)
    kc = lane & (GRID_W - 1)
    hi = lane >= GRID_W
    d = kc - q + (NA_WIN_C - 1)
    cs = jnp.clip(q - NA_WIN_C // 2, 0, GRID_W - NA_WIN_C)
    inwin = (kc >= cs) & (kc < cs + NA_WIN_C)
    base = h * (N_RO * N_CO)
    for ro in range(N_RO - 1):
        acc = jnp.full(shape, NEG, F32)
        for dd in range(N_CO):
            s0 = rpb_ref[base + ro * N_CO + dd]
            s1 = rpb_ref[base + (ro + 1) * N_CO + dd]
            acc = jnp.where(d == dd, jnp.where(hi, s1, s0), acc)
        o_ref[ro] = jnp.where(inwin, acc, NEG)


def _nabias_call(rpb):
    return pl.pallas_call(
        _nabias_kernel,
        grid=(NA_HEADS,),
        in_specs=[pl.BlockSpec(memory_space=pltpu.SMEM)],
        out_specs=pl.BlockSpec((None, N_RO - 1, GRID_W, LANES), lambda h: (h, 0, 0, 0)),
        out_shape=jax.ShapeDtypeStruct((NA_HEADS, N_RO - 1, GRID_W, LANES), F32),
        compiler_params=_cparams(("arbitrary",)),
        name="na_bias",
    )(rpb.reshape(-1))


def _mod_norm(x, g, shift, scale):
    ms = jnp.mean(x * x, axis=-1, keepdims=True)
    return (x * lax.rsqrt(ms + EPS) * g) * (1.0 + scale) + shift


def _inproj_kernel(x_ref, mod_ref, g1_ref, w_ref, gw_ref, gb_ref, cos_ref, sin_ref,
                   na_ref, wa_ref, gqk_ref, gv_ref, go_ref, la_ref):
    mod = mod_ref[...]
    h = _mod_norm(x_ref[...], g1_ref[...], mod[:, 0:D_MODEL], mod[:, D_MODEL:2 * D_MODEL])
    hb = h.astype(BF16)

    def proj(c0, c1):
        return jnp.dot(hb, w_ref[:, c0:c1], preferred_element_type=F32)

    na_ref[...] = proj(C_NA, C_WAQ).astype(BF16)
    cos = cos_ref[...]
    sin = sin_ref[...]
    wa_ref[:, 0:256] = (proj(C_WAQ, C_WAQS) * cos + proj(C_WAQS, C_WAK) * sin).astype(BF16)
    wa_ref[:, 256:384] = (proj(C_WAK, C_WAKS) * cos[:, 0:LANES]
                          + proj(C_WAKS, C_WAV) * sin[:, 0:LANES]).astype(BF16)
    wa_ref[:, 384:512] = proj(C_WAV, C_GQ).astype(BF16)
    gqk_ref[...] = proj(C_GQ, C_GV)
    gv_ref[...] = proj(C_GV, C_GO).astype(BF16)
    go_ref[...] = proj(C_GO, C_LR)
    lr = proj(C_LR, IN_COLS).astype(BF16)
    logit = jnp.dot(lr, gw_ref[...], preferred_element_type=F32) + gb_ref[...]
    log_sig = jnp.minimum(logit, 0.0) - jnp.log1p(jnp.exp(-jnp.abs(logit)))
    la_ref[...] = log_sig * (1.0 / GLA_GATE_TAU)


def _inproj_call(x_all, mod_rows, g1, w_cat, gw, gb, cos_t, sin_t):
    B = x_all.shape[0]
    tile = lambda w: pl.BlockSpec((None, TILE, w), lambda b, i: (b, i, 0))
    const = lambda r, c: pl.BlockSpec((r, c), lambda b, i: (0, 0))
    return pl.pallas_call(
        _inproj_kernel,
        grid=(B, N_TILES),
        in_specs=[
            tile(D_MODEL),
            pl.BlockSpec((None, 1, 6 * D_MODEL), lambda b, i: (2 * b + jnp.minimum(i, 1), 0, 0)),
            const(1, D_MODEL),
            const(D_MODEL, IN_COLS),
            const(LANES, 2 * GLA_QK_WIDTH),
            const(1, 2 * GLA_QK_WIDTH),
            pl.BlockSpec((TILE, 256), lambda b, i: (i, 0)),
            pl.BlockSpec((TILE, 256), lambda b, i: (i, 0)),
        ],
        out_specs=[tile(768), tile(512), tile(512), tile(512), tile(512), tile(512)],
        out_shape=[
            jax.ShapeDtypeStruct((B, TOK, 768), BF16),
            jax.ShapeDtypeStruct((B, TOK, 512), BF16),
            jax.ShapeDtypeStruct((B, TOK, 512), F32),
            jax.ShapeDtypeStruct((B, TOK, 512), BF16),
            jax.ShapeDtypeStruct((B, TOK, 512), F32),
            jax.ShapeDtypeStruct((B, TOK, 512), F32),
        ],
        compiler_params=_cparams(("arbitrary", "arbitrary")),
        name="in_proj",
    )(x_all, mod_rows, g1, w_cat, gw, gb, cos_t, sin_t)


def _split_heads_rows(q):
    lo = _lane_lo(q.shape)
    zero = jnp.zeros_like(q)
    return jnp.concatenate([jnp.where(lo, q, zero), jnp.where(lo, zero, q)], axis=0)


def _merge_heads_rows(o, n):
    return jnp.where(_lane_lo((n, LANES)), o[0:n], o[n:2 * n])


def _na_kernel(slab_ref, bias_ref, o_ref, *, row0):
    r = pl.program_id(1)
    n_win = NA_WIN_R * GRID_W

    def ctx_scores(q2, p):
        kc = slab_ref[0:CTX_LEN, 256 + p * LANES:256 + (p + 1) * LANES]
        return lax.dot_general(q2, kc, NT_DIMS, preferred_element_type=F32)

    def ctx_values(p):
        return slab_ref[0:CTX_LEN, 512 + p * LANES:512 + (p + 1) * LANES]

    @pl.when(r < GRID_ROWS)
    def _():
        rs = jnp.clip(r - NA_WIN_R // 2, 0, GRID_ROWS - NA_WIN_R)
        qrow = pl.multiple_of(CTX_LEN + r * GRID_W, GRID_W)
        krow = pl.multiple_of(CTX_LEN + rs * GRID_W, GRID_W)
        ro0 = rs - r + (NA_WIN_R - 1)
        for p in range(NA_HEADS // 2):
            q2 = _split_heads_rows(slab_ref[pl.ds(qrow, GRID_W), p * LANES:(p + 1) * LANES])
            kw = slab_ref[pl.ds(krow, n_win), 256 + p * LANES:256 + (p + 1) * LANES]
            vw = slab_ref[pl.ds(krow, n_win), 512 + p * LANES:512 + (p + 1) * LANES]
            bias = jnp.concatenate([
                jnp.concatenate([bias_ref[2 * p + a, ro0 + 2 * jj] for jj in range(NA_WIN_R // 2)], axis=1)
                for a in range(2)], axis=0)
            s_w = lax.dot_general(q2, kw, NT_DIMS, preferred_element_type=F32) + bias
            s_c = ctx_scores(q2, p)
            m = jnp.maximum(jnp.max(s_w, axis=-1, keepdims=True), jnp.max(s_c, axis=-1, keepdims=True))
            p_w = jnp.exp(s_w - m)
            p_c = jnp.exp(s_c - m)
            den = jnp.sum(p_w, axis=-1, keepdims=True) + jnp.sum(p_c, axis=-1, keepdims=True)
            o = (jnp.dot(p_w.astype(BF16), vw, preferred_element_type=F32)
                 + jnp.dot(p_c.astype(BF16), ctx_values(p), preferred_element_type=F32)) / den
            o_ref[:, p * LANES:(p + 1) * LANES] = _merge_heads_rows(o, GRID_W).astype(BF16)

    @pl.when(r >= GRID_ROWS)
    def _():
        qrow = pl.multiple_of((r - GRID_ROWS) * GRID_W, GRID_W)
        for p in range(NA_HEADS // 2):
            q2 = _split_heads_rows(slab_ref[pl.ds(qrow, GRID_W), p * LANES:(p + 1) * LANES])
            s_c = ctx_scores(q2, p)
            m = jnp.max(s_c, axis=-1, keepdims=True)
            p_c = jnp.exp(s_c - m)
            den = jnp.sum(p_c, axis=-1, keepdims=True)
            o = jnp.dot(p_c.astype(BF16), ctx_values(p), preferred_element_type=F32) / den
            o_ref[:, p * LANES:(p + 1) * LANES] = _merge_heads_rows(o, GRID_W).astype(BF16)


def _na_call(na, bias, with_ctx):
    B = na.shape[0]
    n_ctx = CTX_LEN // GRID_W if with_ctx else 0
    out_rows = SEQ + n_ctx * GRID_W

    def out_map(b, r):
        if with_ctx:
            return (b, jnp.where(r < GRID_ROWS, r + n_ctx, r - GRID_ROWS), 0)
        return (b, r, 0)

    return pl.pallas_call(
        functools.partial(_na_kernel, row0=0),
        grid=(B, GRID_ROWS + n_ctx),
        in_specs=[
            pl.BlockSpec((None, TOK, 768), lambda b, r: (b, 0, 0)),
            pl.BlockSpec((NA_HEADS, N_RO - 1, GRID_W, LANES), lambda b, r: (0, 0, 0, 0)),
        ],
        out_specs=pl.BlockSpec((None, GRID_W, NA_WIDTH), out_map),
        out_shape=jax.ShapeDtypeStruct((B, out_rows, NA_WIDTH), BF16),
        compiler_params=_cparams(("arbitrary", "arbitrary")),
        name="na_attn",
    )(na, bias)


N_WA_BLOCKS = SEQ // WA_BLOCK


def _wa_kernel(sink_ref, slab_ref, o_ref):
    n = pl.program_id(1)
    kcol = slice(256, 384)
    vcol = slice(384, 512)
    kc = slab_ref[0:CTX_LEN, kcol]
    vc = slab_ref[0:CTX_LEN, vcol]
    lo = _lane_lo((WA_BLOCK, LANES))

    def softmax_pv(scores, values, sink):
        m = sink
        for s in scores:
            m = jnp.maximum(m, jnp.max(s, axis=-1, keepdims=True))
        den = jnp.exp(sink - m)
        o = None
        for s, v in zip(scores, values):
            e = jnp.exp(s - m)
            den = den + jnp.sum(e, axis=-1, keepdims=True)
            t = jnp.dot(e.astype(BF16), v, preferred_element_type=F32)
            o = t if o is None else o + t
        return o / den

    def heads(qa, qb):
        zero = jnp.zeros_like(qa)
        for slot, src in ((0, qa), (1, qa), (2, qb), (3, qb)):
            low = slot % 2 == 0
            q = jnp.where(lo, src, zero) if low else jnp.where(lo, zero, src)
            yield q, sink_ref[WA_HEAD_ORDER[slot]], slot

    def store(outs):
        o_ref[:, 0:LANES] = jnp.where(lo, outs[0], outs[1]).astype(BF16)
        o_ref[:, LANES:2 * LANES] = jnp.where(lo, outs[2], outs[3]).astype(BF16)

    @pl.when(n < N_WA_BLOCKS)
    def _():
        qrow = pl.multiple_of(CTX_LEN + n * WA_BLOCK, WA_BLOCK)
        prow = pl.multiple_of(qrow - WA_BLOCK, WA_BLOCK)
        nrow = pl.multiple_of(jnp.minimum(qrow + WA_BLOCK, TOK - WA_BLOCK), WA_BLOCK)
        k_blocks = [slab_ref[pl.ds(s, WA_BLOCK), kcol] for s in (prow, qrow, nrow)]
        v_blocks = [slab_ref[pl.ds(s, WA_BLOCK), vcol] for s in (prow, qrow, nrow)]
        i = lax.broadcasted_iota(jnp.int32, (WA_BLOCK, WA_BLOCK), 0)
        j = lax.broadcasted_iota(jnp.int32, (WA_BLOCK, WA_BLOCK), 1)
        keep_prev = (j >= i) & (n > 0)
        keep_next = (j <= i) & (n < N_WA_BLOCKS - 1)
        qa = slab_ref[pl.ds(qrow, WA_BLOCK), 0:LANES]
        qb = slab_ref[pl.ds(qrow, WA_BLOCK), LANES:2 * LANES]
        outs = []
        for q, sink, _ in heads(qa, qb):
            s_p = jnp.where(keep_prev, lax.dot_general(q, k_blocks[0], NT_DIMS, preferred_element_type=F32), NEG)
            s_m = lax.dot_general(q, k_blocks[1], NT_DIMS, preferred_element_type=F32)
            s_n = jnp.where(keep_next, lax.dot_general(q, k_blocks[2], NT_DIMS, preferred_element_type=F32), NEG)
            s_c = lax.dot_general(q, kc, NT_DIMS, preferred_element_type=F32)
            outs.append(softmax_pv([s_p, s_m, s_n, s_c], v_blocks + [vc], sink))
        store(outs)

    @pl.when(n >= N_WA_BLOCKS)
    def _():
        qrow = pl.multiple_of((n - N_WA_BLOCKS) * WA_BLOCK, WA_BLOCK)
        qa = slab_ref[pl.ds(qrow, WA_BLOCK), 0:LANES]
        qb = slab_ref[pl.ds(qrow, WA_BLOCK), LANES:2 * LANES]
        outs = []
        for q, sink, _ in heads(qa, qb):
            s_c = lax.dot_general(q, kc, NT_DIMS, preferred_element_type=F32)
            outs.append(softmax_pv([s_c], [vc], sink))
        store(outs)


def _wa_call(wa, sink, with_ctx):
    B = wa.shape[0]
    n_ctx = CTX_LEN // WA_BLOCK if with_ctx else 0
    out_rows = SEQ + n_ctx * WA_BLOCK

    def out_map(b, n):
        if with_ctx:
            return (b, jnp.where(n < N_WA_BLOCKS, n + n_ctx, n - N_WA_BLOCKS), 0)
        return (b, n, 0)

    return pl.pallas_call(
        _wa_kernel,
        grid=(B, N_WA_BLOCKS + n_ctx),
        in_specs=[
            pl.BlockSpec(memory_space=pltpu.SMEM),
            pl.BlockSpec((None, TOK, 512), lambda b, n: (b, 0, 0)),
        ],
        out_specs=pl.BlockSpec((None, WA_BLOCK, WA_Q_WIDTH), out_map),
        out_shape=jax.ShapeDtypeStruct((B, out_rows, WA_Q_WIDTH), BF16),
        compiler_params=_cparams(("arbitrary", "arbitrary")),
        name="wa_attn",
    )(sink, wa)


GLA_LEVELS = int(math.log2(GLA_CHUNK))
ONES_ROWS = 16
E_ROWS = (2 + 2 * GLA_LEVELS) * GLA_CHUNK + ONES_ROWS


def _gla_constants():
    C = GLA_CHUNK
    p = np.arange(C)
    mats = [(p[None, :] <= p[:, None]), (p[None, :] > p[:, None])]
    masks = []
    for lvl in range(GLA_LEVELS):
        m = C >> (lvl + 1)
        pair = p // (2 * m)
        half = (p // m) % 2
        a_end = pair * 2 * m + m - 1
        r = p[None, :]
        mats.append((half[:, None] == 1) & (r > a_end[:, None]) & (r <= p[:, None]))
        mats.append((half[:, None] == 0) & (r > p[:, None]) & (r <= a_end[:, None]))
        masks.append((pair[:, None] == pair[None, :]) & (half[:, None] == 1) & (half[None, :] == 0))
    masks.append(p[:, None] == p[None, :])
    mats.append(np.ones((ONES_ROWS, C), bool))
    fwd = np.concatenate(mats, axis=0).astype(np.float32)
    msk = np.stack(masks).astype(np.float32)
    bwd_blocks = [blk[::-1, ::-1] for blk in mats[:-1]] + [mats[-1]]
    bwd = np.concatenate(bwd_blocks, axis=0).astype(np.float32)
    msk_b = msk[:, ::-1, ::-1]
    emat = np.stack([fwd, bwd])
    lmask = np.stack([msk, msk_b])
    lmask = np.concatenate([lmask, lmask], axis=-1)
    return emat, lmask


def _gla_kernel(emat_ref, lmask_ref, qkf_ref, laf_ref, vf_ref, qkb_ref, lab_ref, vb_ref,
                of_ref, ob_ref, st_ref):
    i = pl.program_id(1)

    @pl.when(i == 0)
    def _():
        st_ref[...] = jnp.zeros_like(st_ref)

    C = GLA_CHUNK
    lo = _lane_lo((C, LANES))
    v_lo = _lane_lo((C, 2 * GLA_DV), width=2 * GLA_DV)
    st_rows = lax.broadcasted_iota(jnp.int32, (2 * GLA_DV, LANES), 0) < GLA_DV
    st_keep = st_rows == _lane_lo((2 * GLA_DV, LANES))

    def chunk(d, p, r0, qk_ref, la_ref, v_ref, o_ref):
        rows = slice(r0, r0 + C)
        g = la_ref[rows, p * LANES:(p + 1) * LANES]
        q = qk_ref[rows, p * LANES:(p + 1) * LANES]
        k = qk_ref[rows, GLA_QK_WIDTH + p * LANES:GLA_QK_WIDTH + (p + 1) * LANES]
        v = v_ref[rows, p * 2 * GLA_DV:(p + 1) * 2 * GLA_DV]
        g_hi = g.astype(BF16)
        g_lo = (g - g_hi.astype(F32)).astype(BF16)
        e2 = jnp.dot(emat_ref[d], jnp.concatenate([g_hi, g_lo], axis=1), preferred_element_type=F32)
        e = e2[:, 0:LANES] + e2[:, LANES:2 * LANES]
        b_inc = e[0:C]
        b_suf = e[C:2 * C]
        b_end = e[E_ROWS - ONES_ROWS:E_ROWS - ONES_ROWS + 1]
        st = st_ref[d, p]
        o = lax.dot_general((q * jnp.exp(b_inc)).astype(BF16), st.astype(BF16), NT_DIMS,
                            preferred_element_type=F32)

        def head_block(kk):
            zero = jnp.zeros_like(kk)
            return jnp.concatenate([jnp.where(lo, kk, zero), jnp.where(lo, zero, kk)], axis=0)

        att = lax.dot_general(q.astype(BF16), head_block(k.astype(BF16)), NT_DIMS,
                              preferred_element_type=F32) * lmask_ref[d, GLA_LEVELS]
        for lvl in range(GLA_LEVELS):
            eq = e[(2 + 2 * lvl) * C:(3 + 2 * lvl) * C]
            ek = e[(3 + 2 * lvl) * C:(4 + 2 * lvl) * C]
            ql = (q * jnp.exp(eq)).astype(BF16)
            kl = (k * jnp.exp(ek)).astype(BF16)
            att = att + lax.dot_general(ql, head_block(kl), NT_DIMS,
                                        preferred_element_type=F32) * lmask_ref[d, lvl]
        vzero = jnp.zeros_like(v)
        v_bd = jnp.concatenate([jnp.where(v_lo, v, vzero), jnp.where(v_lo, vzero, v)], axis=0)
        o = o + jnp.dot(att.astype(BF16), v_bd, preferred_element_type=F32)
        o_ref[rows, p * 2 * GLA_DV:(p + 1) * 2 * GLA_DV] = o
        ks = (k * jnp.exp(b_suf)).astype(BF16)
        upd = lax.dot_general(v, ks, TN_DIMS, preferred_element_type=F32)
        st_ref[d, p] = st * jnp.exp(b_end) + jnp.where(st_keep, upd, 0.0)

    n_chunks = TILE // C
    for step in range(n_chunks):
        for p in range(GLA_HEADS // 2):
            chunk(0, p, step * C, qkf_ref, laf_ref, vf_ref, of_ref)
            chunk(1, p, (n_chunks - 1 - step) * C, qkb_ref, lab_ref, vb_ref, ob_ref)


def _gla_call(gqk, la, gv, emat, lmask):
    B = gqk.shape[0]
    fwd = lambda b, i: (b, i, 0)
    bwd = lambda b, i: (b, jnp.where(i == 0, 0, N_TILES - i), 0)
    bwd_la = lambda b, i: (b, jnp.where(i == 0, 0, N_TILES - i), 1)
    return pl.pallas_call(
        _gla_kernel,
        grid=(B, N_TILES),
        in_specs=[
            pl.BlockSpec((2, E_ROWS, GLA_CHUNK), lambda b, i: (0, 0, 0)),
            pl.BlockSpec((2, GLA_LEVELS + 1, GLA_CHUNK, LANES), lambda b, i: (0, 0, 0, 0)),
            pl.BlockSpec((None, TILE, 512), fwd),
            pl.BlockSpec((None, TILE, 256), fwd),
            pl.BlockSpec((None, TILE, 512), fwd),
            pl.BlockSpec((None, TILE, 512), bwd),
            pl.BlockSpec((None, TILE, 256), bwd_la),
            pl.BlockSpec((None, TILE, 512), bwd),
        ],
        out_specs=[pl.BlockSpec((None, TILE, 512), fwd), pl.BlockSpec((None, TILE, 512), bwd)],
        out_shape=[jax.ShapeDtypeStruct((B, TOK, 512), F32)] * 2,
        scratch_shapes=[pltpu.VMEM((2, GLA_HEADS // 2, 2 * GLA_DV, LANES), F32)],
        compiler_params=_cparams(("arbitrary", "arbitrary")),
        name="gla_scan",
    )(emat, lmask, gqk, la, gv, gqk, la, gv)


def _outproj_kernel(x_ref, mod_ref, na_ref, wa_ref, of_ref, ob_ref, go_ref, gn_ref, w_ref, o_ref):
    og = of_ref[...] + ob_ref[...]
    gn = gn_ref[...]
    parts = []
    for h in range(GLA_HEADS):
        seg = og[:, h * GLA_DV:(h + 1) * GLA_DV]
        ms = jnp.mean(seg * seg, axis=-1, keepdims=True)
        parts.append(seg * lax.rsqrt(ms + EPS) * gn)
    go = go_ref[...]
    y_g = (jnp.concatenate(parts, axis=1) * (go * jax.nn.sigmoid(go))).astype(BF16)
    res = (jnp.dot(na_ref[...], w_ref[0:256, :], preferred_element_type=F32)
           + jnp.dot(wa_ref[...], w_ref[256:512, :], preferred_element_type=F32)
           + jnp.dot(y_g, w_ref[512:1024, :], preferred_element_type=F32))
    gate = mod_ref[...][:, 2 * D_MODEL:3 * D_MODEL]
    o_ref[...] = x_ref[...] + gate * res


def _outproj_call(x_all, mod_rows, o_na, o_wa, o_f, o_b, go, gn, w_out, with_ctx):
    B = x_all.shape[0]
    t0 = 0 if with_ctx else 1
    n_t = N_TILES - t0
    full = lambda w: pl.BlockSpec((None, TILE, w), lambda b, i: (b, i + t0, 0))
    own = lambda w: pl.BlockSpec((None, TILE, w), lambda b, i: (b, i, 0))
    return pl.pallas_call(
        _outproj_kernel,
        grid=(B, n_t),
        in_specs=[
            full(D_MODEL),
            pl.BlockSpec((None, 1, 6 * D_MODEL), lambda b, i: (2 * b + jnp.minimum(i + t0, 1), 0, 0)),
            own(NA_WIDTH), own(WA_Q_WIDTH), full(512), full(512), full(512),
            pl.BlockSpec((1, GLA_DV), lambda b, i: (0, 0)),
            pl.BlockSpec((MIX_WIDTH, D_MODEL), lambda b, i: (0, 0)),
        ],
        out_specs=own(D_MODEL),
        out_shape=jax.ShapeDtypeStruct((B, n_t * TILE, D_MODEL), F32),
        compiler_params=_cparams(("arbitrary", "arbitrary")),
        name="out_proj",
    )(x_all, mod_rows, o_na, o_wa, o_f, o_b, go, gn, w_out)


HALO = SUBLANES


def _ffn_kernel(x_ref, xp_ref, xn_ref, mod_ref, g2_ref, wv_ref, wg_ref, cw_ref, cb_ref, wd_ref,
                fg_ref, o_ref, gs_ref, *, t0, final):
    ti = pl.program_id(1) + t0
    mod = mod_ref[...]
    shift, scale = mod[:, 3 * D_MODEL:4 * D_MODEL], mod[:, 4 * D_MODEL:5 * D_MODEL]
    g2 = g2_ref[...]
    x = x_ref[...]
    h_t = _mod_norm(x, g2, shift, scale)
    h_ext = jnp.concatenate([_mod_norm(xp_ref[...], g2, shift, scale), h_t,
                             _mod_norm(xn_ref[...], g2, shift, scale)], axis=0).astype(BF16)
    h_b = h_t.astype(BF16)
    seq_start = ti <= 1
    seq_end = (ti == 0) | (ti == N_TILES - 1)
    acc = jnp.zeros((TILE, D_MODEL), F32)
    for c in range(FF_DIM // FF_CHUNK):
        cols = slice(c * FF_CHUNK, (c + 1) * FF_CHUNK)
        g_ext = jnp.dot(h_ext, wg_ref[:, cols], preferred_element_type=F32)
        gs_ref[...] = g_ext
        gs_ref[0:HALO, :] = jnp.where(seq_start, 0.0, g_ext[0:HALO])
        gs_ref[HALO + TILE:, :] = jnp.where(seq_end, 0.0, g_ext[HALO + TILE:])
        cw = cw_ref[:, cols]
        gate = (cw[0:1] * gs_ref[pl.ds(HALO - 1, TILE), :] + cw[1:2] * gs_ref[pl.ds(HALO, TILE), :]
                + cw[2:3] * gs_ref[pl.ds(HALO + 1, TILE), :] + cb_ref[:, cols])
        val = jnp.dot(h_b, wv_ref[:, cols], preferred_element_type=F32)
        act = 0.5 * gate * (1.0 + lax.erf(gate * (1.0 / math.sqrt(2.0))))
        acc = acc + jnp.dot((act * val).astype(BF16), wd_ref[cols, :], preferred_element_type=F32)
    y = x + mod[:, 5 * D_MODEL:6 * D_MODEL] * acc
    if final:
        ms = jnp.mean(y * y, axis=-1, keepdims=True)
        y = y * lax.rsqrt(ms + EPS) * fg_ref[...]
    o_ref[...] = y


def _ffn_call(x1, mod_rows, g2, wv, wg, cw, cb, wd, fg, with_ctx, final):
    B = x1.shape[0]
    t0 = 0 if with_ctx else 1
    n_t = x1.shape[1] // TILE
    per8 = TILE // HALO
    last8 = x1.shape[1] // HALO - 1
    const = lambda r, c: pl.BlockSpec((r, c), lambda b, i: (0, 0))
    return pl.pallas_call(
        functools.partial(_ffn_kernel, t0=t0, final=final),
        grid=(B, n_t),
        in_specs=[
            pl.BlockSpec((None, TILE, D_MODEL), lambda b, i: (b, i, 0)),
            pl.BlockSpec((None, HALO, D_MODEL), lambda b, i: (b, jnp.maximum(i * per8 - 1, 0), 0)),
            pl.BlockSpec((None, HALO, D_MODEL), lambda b, i: (b, jnp.minimum((i + 1) * per8, last8), 0)),
            pl.BlockSpec((None, 1, 6 * D_MODEL), lambda b, i: (2 * b + jnp.minimum(i + t0, 1), 0, 0)),
            const(1, D_MODEL),
            const(D_MODEL, FF_DIM), const(D_MODEL, FF_DIM),
            const(3, FF_DIM), const(1, FF_DIM),
            const(FF_DIM, D_MODEL),
            const(1, D_MODEL),
        ],
        out_specs=pl.BlockSpec((None, TILE, D_MODEL), lambda b, i: (b, i, 0)),
        out_shape=jax.ShapeDtypeStruct(x1.shape, F32),
        scratch_shapes=[pltpu.VMEM((TILE + 2 * HALO, FF_CHUNK), F32)],
        compiler_params=_cparams(("arbitrary", "arbitrary")),
        name="conv_ffn",
    )(x1, x1, x1, mod_rows, g2, wv, wg, cw, cb, wd, fg)


def _rope_tables():
    t = np.arange(SEQ)
    n_freq = WA_HEAD_DIM // 4
    inv = ROPE_THETA ** (-np.arange(n_freq, dtype=np.float32) / n_freq)
    ang = np.concatenate([(t // GRID_W).astype(np.float32)[:, None] * inv,
                          (t % GRID_W).astype(np.float32)[:, None] * inv], axis=-1)
    cos = np.tile(np.cos(ang), (1, 2 * WA_HEADS))
    sin = np.tile(np.sin(ang), (1, 2 * WA_HEADS))
    cos = np.concatenate([np.ones((CTX_LEN, 256), np.float32), cos], axis=0)
    sin = np.concatenate([np.zeros((CTX_LEN, 256), np.float32), sin], axis=0)
    return jnp.asarray(cos, F32), jnp.asarray(sin, F32)


def _rotate_half_cols(w, n_heads):
    w = w.reshape(w.shape[0], n_heads, 2, WA_HEAD_DIM // 2)
    return jnp.stack([-w[:, :, 1], w[:, :, 0]], axis=2).reshape(w.shape[0], n_heads * WA_HEAD_DIM)


def _layer_weights(w_in, gate_w, gate_b, w_out):
    pts = np.cumsum([0, NA_WIDTH, NA_WIDTH, NA_WIDTH, WA_Q_WIDTH, WA_KV_WIDTH, WA_KV_WIDTH,
                     GLA_QK_WIDTH, GLA_QK_WIDTH, GLA_V_WIDTH, GLA_V_WIDTH, 2 * GLA_GATE_RANK])
    seg = lambda k: w_in[:, pts[k]:pts[k + 1]]
    scale = NA_HEAD_DIM ** -0.5
    waq = seg(3).reshape(D_MODEL, WA_HEADS, WA_HEAD_DIM)[:, np.array(WA_HEAD_ORDER)].reshape(D_MODEL, WA_Q_WIDTH) * scale
    wak = seg(4)
    lr_pad = jnp.zeros((D_MODEL, IN_COLS - C_LR - 2 * GLA_GATE_RANK), w_in.dtype)
    w_cat = jnp.concatenate([
        seg(0) * scale, seg(1), seg(2),
        waq, _rotate_half_cols(waq, WA_HEADS), wak, _rotate_half_cols(wak, WA_KV_HEADS), seg(5),
        seg(6) * scale, seg(7), seg(8), seg(9), seg(10), lr_pad], axis=1).astype(BF16)
    gw = jnp.zeros((LANES, 2 * GLA_QK_WIDTH), F32)
    gw = gw.at[0:GLA_GATE_RANK, 0:GLA_QK_WIDTH].set(gate_w[0])
    gw = gw.at[GLA_GATE_RANK:2 * GLA_GATE_RANK, GLA_QK_WIDTH:].set(gate_w[1])
    gb = gate_b.reshape(1, 2 * GLA_QK_WIDTH)
    wo_wa = w_out[NA_WIDTH:NA_WIDTH + WA_Q_WIDTH].reshape(WA_HEADS, WA_HEAD_DIM, D_MODEL)[np.array(WA_HEAD_ORDER)]
    wo = jnp.concatenate([w_out[:NA_WIDTH], wo_wa.reshape(WA_Q_WIDTH, D_MODEL),
                          w_out[NA_WIDTH + WA_Q_WIDTH:]], axis=0).astype(BF16)
    return w_cat, gw.astype(BF16), gb, wo


def kernel(x, c, ctx, c_ctx, w_mod, b_mod, norm1_g, norm2_g, w_in, na_rpb, wa_sink, gla_gate_w,
           gla_gate_b, gla_norm_g, w_out, ffn_w_up, ffn_conv_w, ffn_conv_b, ffn_w_down, final_norm_g):
    B = x.shape[0]
    assert x.shape == (B, SEQ, D_MODEL) and ctx.shape == (B, CTX_LEN, D_MODEL)
    n_c = -(-(B + 1) // SUBLANES) * SUBLANES
    cc = jnp.concatenate([c, c_ctx[None], jnp.zeros((n_c - B - 1, D_MODEL), F32)], axis=0)
    mods = _mod_call(cc, w_mod, b_mod)
    cos_t, sin_t = _rope_tables()
    emat, lmask = _gla_constants()
    emat = jnp.asarray(emat, BF16)
    lmask = jnp.asarray(lmask, F32)
    fg = final_norm_g.reshape(1, D_MODEL)

    x_all = jnp.concatenate([ctx, x], axis=1)
    out = None
    for l in range(DEPTH):
        last = l == DEPTH - 1
        mod_rows = jnp.stack([jnp.broadcast_to(mods[l, B], (B, 6 * D_MODEL)), mods[l, :B]],
                             axis=1).reshape(2 * B, 1, 6 * D_MODEL)
        w_cat, gw, gb, wo = _layer_weights(w_in[l], gla_gate_w[l], gla_gate_b[l], w_out[l])
        na, wa, gqk, gv, go, la = _inproj_call(x_all, mod_rows, norm1_g[l].reshape(1, D_MODEL),
                                               w_cat, gw, gb, cos_t, sin_t)
        bias = _nabias_call(na_rpb[l])
        o_na = _na_call(na, bias, with_ctx=not last)
        o_wa = _wa_call(wa, wa_sink[l], with_ctx=not last)
        o_f, o_b = _gla_call(gqk, la, gv, emat, lmask)
        x1 = _outproj_call(x_all, mod_rows, o_na, o_wa, o_f, o_b, go,
                           gla_norm_g[l].reshape(1, GLA_DV), wo, with_ctx=not last)
        w_up = ffn_w_up[l].astype(BF16)
        x2 = _ffn_call(x1, mod_rows, norm2_g[l].reshape(1, D_MODEL), w_up[:, :FF_DIM], w_up[:, FF_DIM:],
                       ffn_conv_w[l], ffn_conv_b[l].reshape(1, FF_DIM), ffn_w_down[l].astype(BF16), fg,
                       with_ctx=not last, final=last)
        if last:
            out = x2
        else:
            x_all = x2
    return out
```

```python
import functools
import math

import numpy as np
import jax
import jax.numpy as jnp
from jax import lax
from jax.experimental import pallas as pl
from jax.experimental.pallas import tpu as pltpu

D_MODEL = 1024
SEQ = 2048
DEPTH = 2
CTX_LEN = 256
GRID_W = 64
GRID_ROWS = SEQ // GRID_W
TOK = CTX_LEN + SEQ

NA_HEADS = 4
NA_HEAD_DIM = 64
NA_WIN_R = 8
NA_WIN_C = 16
WA_HEADS = 4
WA_KV_HEADS = 2
WA_HEAD_DIM = 64
WA_WINDOW = 128
WA_BLOCK = 128
GLA_HEADS = 4
GLA_DK = 64
GLA_DV = 128
GLA_GATE_RANK = 16
GLA_GATE_TAU = 16.0
GLA_CHUNK = 64

NA_WIDTH = NA_HEADS * NA_HEAD_DIM
WA_Q_WIDTH = WA_HEADS * WA_HEAD_DIM
WA_KV_WIDTH = WA_KV_HEADS * WA_HEAD_DIM
GLA_QK_WIDTH = GLA_HEADS * GLA_DK
GLA_V_WIDTH = GLA_HEADS * GLA_DV
MIX_WIDTH = NA_WIDTH + WA_Q_WIDTH + GLA_V_WIDTH
IN_WIDTH = 3 * NA_WIDTH + WA_Q_WIDTH + 2 * WA_KV_WIDTH + 2 * GLA_QK_WIDTH + 2 * GLA_V_WIDTH + 2 * GLA_GATE_RANK

FF_DIM = 2816
FF_CHUNK = 1408
ROPE_THETA = 10000.0
EPS = 1e-6

TILE = 256
N_TILES = TOK // TILE
LANES = 128
SUBLANES = 8
NEG = -1e30
VMEM_LIMIT = 56 * 1024 * 1024

F32 = jnp.float32
BF16 = jnp.bfloat16
NT_DIMS = (((1,), (1,)), ((), ()))
TN_DIMS = (((0,), (0,)), ((), ()))

W_NA = 0
W_WAK = 1024
W_WAV = 1152
W_GQ = 1280
W_GK = 1536
W_GV = 1792
W_GO = 2304
W_LR = 2816
X_WAQ = 0
X_WAQS = 256
X_WAKS = 512
X_LR = 640
X_COLS = 768

WA_HEAD_ORDER = (0, 2, 1, 3)
Q_SCALE = NA_HEAD_DIM ** -0.5


def _cparams(sem):
    return pltpu.CompilerParams(dimension_semantics=sem, vmem_limit_bytes=VMEM_LIMIT)


def _lane_lo(shape, width=LANES):
    lane = lax.broadcasted_iota(jnp.int32, shape, len(shape) - 1)
    return (lane & (width - 1)) < (width // 2)


def _mod_kernel(c_ref, w_ref, b_ref, o_ref):
    c = c_ref[...]
    sc = c * jax.nn.sigmoid(c)
    o_ref[...] = jnp.dot(sc.astype(BF16), w_ref[...].astype(BF16),
                         preferred_element_type=F32) + b_ref[...]


def _mod_call(cc, w_mod, b_mod):
    tn = 1536
    rows = cc.shape[0]
    return pl.pallas_call(
        _mod_kernel,
        grid=(DEPTH, 6 * D_MODEL // tn),
        in_specs=[
            pl.BlockSpec((rows, D_MODEL), lambda l, j: (0, 0)),
            pl.BlockSpec((None, D_MODEL, tn), lambda l, j: (l, 0, j)),
            pl.BlockSpec((None, 1, tn), lambda l, j: (l, 0, j)),
        ],
        out_specs=pl.BlockSpec((None, rows, tn), lambda l, j: (l, 0, j)),
        out_shape=jax.ShapeDtypeStruct((DEPTH, rows, 6 * D_MODEL), F32),
        compiler_params=_cparams(("arbitrary", "arbitrary")),
        name="adaln_mod",
    )(cc, w_mod, b_mod.reshape(DEPTH, 1, 6 * D_MODEL))


N_RO = 2 * NA_WIN_R - 1
N_CO = 2 * NA_WIN_C - 1


def _nabias_kernel(rpb_ref, o_ref):
    h = pl.program_id(0)
    shape = (GRID_W, LANES)
    q = lax.broadcasted_iota(jnp.int32, shape, 0)
    lane = lax.broadcasted_iota(jnp.int32, shape, 1)
    kc = lane & (GRID_W - 1)
    hi = lane >= GRID_W
    d = kc - q + (NA_WIN_C - 1)
    cs = jnp.clip(q - NA_WIN_C // 2, 0, GRID_W - NA_WIN_C)
    inwin = (kc >= cs) & (kc < cs + NA_WIN_C)
    base = h * (N_RO * N_CO)
    for ro in range(N_RO - 1):
        acc = jnp.full(shape, NEG, F32)
        for dd in range(N_CO):
            s0 = rpb_ref[base + ro * N_CO + dd]
            s1 = rpb_ref[base + (ro + 1) * N_CO + dd]
            acc = jnp.where(d == dd, jnp.where(hi, s1, s0), acc)
        o_ref[ro] = jnp.where(inwin, acc, NEG)


def _nabias_call(rpb):
    return pl.pallas_call(
        _nabias_kernel,
        grid=(NA_HEADS,),
        in_specs=[pl.BlockSpec(memory_space=pltpu.SMEM)],
        out_specs=pl.BlockSpec((None, N_RO - 1, GRID_W, LANES), lambda h: (h, 0, 0, 0)),
        out_shape=jax.ShapeDtypeStruct((NA_HEADS, N_RO - 1, GRID_W, LANES), F32),
        compiler_params=_cparams(("arbitrary",)),
        name="na_bias",
    )(rpb.reshape(-1))


def _mod_norm(x, g, shift, scale):
    ms = jnp.mean(x * x, axis=-1, keepdims=True)
    return (x * lax.rsqrt(ms + EPS) * g) * (1.0 + scale) + shift


def _pick_tile(first_ref, rest_ref):
    return jnp.where(pl.program_id(1) == 0, first_ref[...], rest_ref[...])


def _inproj_kernel(xc_ref, xl_ref, mod_ref, g1_ref, w_ref, wx_ref, gw_ref, gb_ref, cos_ref, sin_ref,
                   na_ref, wa_ref, gqk_ref, gv_ref, go_ref, la_ref):
    mod = mod_ref[...]
    h = _mod_norm(_pick_tile(xc_ref, xl_ref), g1_ref[...], mod[:, 0:D_MODEL], mod[:, D_MODEL:2 * D_MODEL])
    hb = h.astype(BF16)

    def proj(w, c0, c1):
        return jnp.dot(hb, w[:, c0:c1], preferred_element_type=F32)

    na_ref[:, 0:NA_WIDTH] = (proj(w_ref, W_NA, W_NA + NA_WIDTH) * Q_SCALE).astype(BF16)
    na_ref[:, NA_WIDTH:3 * NA_WIDTH] = proj(w_ref, W_NA + NA_WIDTH, W_NA + 3 * NA_WIDTH).astype(BF16)
    cos = cos_ref[...]
    sin = sin_ref[...]
    wa_ref[:, 0:256] = ((proj(wx_ref, X_WAQ, X_WAQS) * cos + proj(wx_ref, X_WAQS, X_WAKS) * sin)
                        * Q_SCALE).astype(BF16)
    wa_ref[:, 256:384] = (proj(w_ref, W_WAK, W_WAV) * cos[:, 0:LANES]
                          + proj(wx_ref, X_WAKS, X_LR) * sin[:, 0:LANES]).astype(BF16)
    wa_ref[:, 384:512] = proj(w_ref, W_WAV, W_GQ).astype(BF16)
    gqk_ref[:, 0:GLA_QK_WIDTH] = proj(w_ref, W_GQ, W_GK) * Q_SCALE
    gqk_ref[:, GLA_QK_WIDTH:] = proj(w_ref, W_GK, W_GV)
    gv_ref[...] = proj(w_ref, W_GV, W_GO).astype(BF16)
    go_ref[...] = proj(w_ref, W_GO, W_LR).astype(BF16)
    lr = proj(wx_ref, X_LR, X_COLS).astype(BF16)
    logit = jnp.dot(lr, gw_ref[...], preferred_element_type=F32) + gb_ref[...]
    log_sig = jnp.minimum(logit, 0.0) - jnp.log1p(jnp.exp(-jnp.abs(logit)))
    la_ref[...] = log_sig * (1.0 / GLA_GATE_TAU)


def _two_source_specs(first, rest, rest_tile0):
    del first, rest
    return (pl.BlockSpec((None, TILE, D_MODEL), lambda b, i: (b, 0, 0)),
            pl.BlockSpec((None, TILE, D_MODEL), lambda b, i: (b, jnp.maximum(i - 1, 0) + rest_tile0, 0)))


def _inproj_call(x_first, x_rest, rest_tile0, mod_rows, g1, w_main, w_extra, gw, gb, cos_t, sin_t):
    B = x_first.shape[0]
    tile = lambda w: pl.BlockSpec((None, TILE, w), lambda b, i: (b, i, 0))
    const = lambda r, c: pl.BlockSpec((r, c), lambda b, i: (0, 0))
    spec_c, spec_l = _two_source_specs(x_first, x_rest, rest_tile0)
    return pl.pallas_call(
        _inproj_kernel,
        grid=(B, N_TILES),
        in_specs=[
            spec_c, spec_l,
            pl.BlockSpec((None, 1, 6 * D_MODEL), lambda b, i: (2 * b + jnp.minimum(i, 1), 0, 0)),
            const(1, D_MODEL),
            const(D_MODEL, IN_WIDTH),
            const(D_MODEL, X_COLS),
            const(LANES, 2 * GLA_QK_WIDTH),
            const(1, 2 * GLA_QK_WIDTH),
            pl.BlockSpec((TILE, 256), lambda b, i: (i, 0)),
            pl.BlockSpec((TILE, 256), lambda b, i: (i, 0)),
        ],
        out_specs=[tile(768), tile(512), tile(512), tile(512), tile(512), tile(512)],
        out_shape=[
            jax.ShapeDtypeStruct((B, TOK, 768), BF16),
            jax.ShapeDtypeStruct((B, TOK, 512), BF16),
            jax.ShapeDtypeStruct((B, TOK, 512), F32),
            jax.ShapeDtypeStruct((B, TOK, 512), BF16),
            jax.ShapeDtypeStruct((B, TOK, 512), BF16),
            jax.ShapeDtypeStruct((B, TOK, 512), F32),
        ],
        compiler_params=_cparams(("arbitrary", "arbitrary")),
        name="in_proj",
    )(x_first, x_rest, mod_rows, g1, w_main, w_extra, gw, gb, cos_t, sin_t)


NA_ROWS_PER_STEP = TILE // GRID_W
NA_STEPS = GRID_ROWS // NA_ROWS_PER_STEP


def _split_heads_rows(q):
    lo = _lane_lo(q.shape)
    zero = jnp.zeros_like(q)
    return jnp.concatenate([jnp.where(lo, q, zero), jnp.where(lo, zero, q)], axis=0)


def _merge_heads_rows(o, n):
    return jnp.where(_lane_lo((n, LANES)), o[0:n], o[n:2 * n])


def _na_kernel(slab_ref, bias_ref, o_ref):
    j = pl.program_id(1)
    n_win = NA_WIN_R * GRID_W

    def ctx_kv(p):
        return (slab_ref[0:CTX_LEN, 256 + p * LANES:256 + (p + 1) * LANES],
                slab_ref[0:CTX_LEN, 512 + p * LANES:512 + (p + 1) * LANES])

    @pl.when(j < NA_STEPS)
    def _():
        qrow = pl.multiple_of(CTX_LEN + j * TILE, TILE)
        for p in range(NA_HEADS // 2):
            kc, vc = ctx_kv(p)
            q_all = slab_ref[pl.ds(qrow, TILE), p * LANES:(p + 1) * LANES]
            q2 = [_split_heads_rows(q_all[a * GRID_W:(a + 1) * GRID_W]) for a in range(NA_ROWS_PER_STEP)]
            s_c = lax.dot_general(jnp.concatenate(q2, axis=0), kc, NT_DIMS, preferred_element_type=F32)
            m_c = jnp.max(s_c, axis=-1, keepdims=True)
            o_w, den_w, m_all = [], [], []
            for a in range(NA_ROWS_PER_STEP):
                r = j * NA_ROWS_PER_STEP + a
                rs = jnp.clip(r - NA_WIN_R // 2, 0, GRID_ROWS - NA_WIN_R)
                krow = pl.multiple_of(CTX_LEN + rs * GRID_W, GRID_W)
                ro0 = rs - r + (NA_WIN_R - 1)
                kw = slab_ref[pl.ds(krow, n_win), 256 + p * LANES:256 + (p + 1) * LANES]
                vw = slab_ref[pl.ds(krow, n_win), 512 + p * LANES:512 + (p + 1) * LANES]
                bias = jnp.concatenate([
                    jnp.concatenate([bias_ref[2 * p + hh, ro0 + 2 * jj] for jj in range(NA_WIN_R // 2)], axis=1)
                    for hh in range(2)], axis=0)
                s_w = lax.dot_general(q2[a], kw, NT_DIMS, preferred_element_type=F32) + bias
                m = jnp.maximum(jnp.max(s_w, axis=-1, keepdims=True), m_c[a * LANES:(a + 1) * LANES])
                p_w = jnp.exp(s_w - m)
                den_w.append(jnp.sum(p_w, axis=-1, keepdims=True))
                o_w.append(jnp.dot(p_w.astype(BF16), vw, preferred_element_type=F32))
                m_all.append(m)
            p_c = jnp.exp(s_c - jnp.concatenate(m_all, axis=0))
            den = jnp.concatenate(den_w, axis=0) + jnp.sum(p_c, axis=-1, keepdims=True)
            o = (jnp.concatenate(o_w, axis=0) + jnp.dot(p_c.astype(BF16), vc, preferred_element_type=F32)) / den
            out = [_merge_heads_rows(o[a * LANES:(a + 1) * LANES], GRID_W) for a in range(NA_ROWS_PER_STEP)]
            o_ref[:, p * LANES:(p + 1) * LANES] = jnp.concatenate(out, axis=0).astype(BF16)

    @pl.when(j >= NA_STEPS)
    def _():
        for p in range(NA_HEADS // 2):
            kc, vc = ctx_kv(p)
            q2 = _split_heads_rows(slab_ref[0:CTX_LEN, p * LANES:(p + 1) * LANES])
            s_c = lax.dot_general(q2, kc, NT_DIMS, preferred_element_type=F32)
            p_c = jnp.exp(s_c - jnp.max(s_c, axis=-1, keepdims=True))
            den = jnp.sum(p_c, axis=-1, keepdims=True)
            o = jnp.dot(p_c.astype(BF16), vc, preferred_element_type=F32) / den
            o_ref[:, p * LANES:(p + 1) * LANES] = _merge_heads_rows(o, CTX_LEN).astype(BF16)


def _na_call(na, bias, with_ctx):
    B = na.shape[0]
    n_ctx = 1 if with_ctx else 0

    def out_map(b, j):
        if with_ctx:
            return (b, jnp.where(j < NA_STEPS, j + 1, 0), 0)
        return (b, j, 0)

    return pl.pallas_call(
        _na_kernel,
        grid=(B, NA_STEPS + n_ctx),
        in_specs=[
            pl.BlockSpec((None, TOK, 768), lambda b, j: (b, 0, 0)),
            pl.BlockSpec((NA_HEADS, N_RO - 1, GRID_W, LANES), lambda b, j: (0, 0, 0, 0)),
        ],
        out_specs=pl.BlockSpec((None, TILE, NA_WIDTH), out_map),
        out_shape=jax.ShapeDtypeStruct((B, SEQ + n_ctx * CTX_LEN, NA_WIDTH), BF16),
        compiler_params=_cparams(("arbitrary", "arbitrary")),
        name="na_attn",
    )(na, bias)


N_WA_BLOCKS = SEQ // WA_BLOCK


def _wa_kernel(sink_ref, slab_ref, o_ref):
    n = pl.program_id(1)
    kcol = slice(256, 384)
    vcol = slice(384, 512)
    kc = slab_ref[0:CTX_LEN, kcol]
    vc = slab_ref[0:CTX_LEN, vcol]
    lo = _lane_lo((WA_BLOCK, LANES))
    rows4 = WA_HEADS * WA_BLOCK
    slot = lax.broadcasted_iota(jnp.int32, (rows4, 1), 0) // WA_BLOCK
    sink = jnp.zeros((rows4, 1), F32)
    for s_i, h in enumerate(WA_HEAD_ORDER):
        sink = jnp.where(slot == s_i, sink_ref[h], sink)

    def stack_heads(qa, qb):
        zero = jnp.zeros_like(qa)
        return jnp.concatenate([jnp.where(lo, qa, zero), jnp.where(lo, zero, qa),
                                jnp.where(lo, qb, zero), jnp.where(lo, zero, qb)], axis=0)

    def softmax_pv(scores, values):
        m = sink
        for s in scores:
            m = jnp.maximum(m, jnp.max(s, axis=-1, keepdims=True))
        den = jnp.exp(sink - m)
        o = None
        for s, v in zip(scores, values):
            e = jnp.exp(s - m)
            den = den + jnp.sum(e, axis=-1, keepdims=True)
            t = jnp.dot(e.astype(BF16), v, preferred_element_type=F32)
            o = t if o is None else o + t
        o = o / den
        b = WA_BLOCK
        o_ref[:, 0:LANES] = jnp.where(lo, o[0:b], o[b:2 * b]).astype(BF16)
        o_ref[:, LANES:2 * LANES] = jnp.where(lo, o[2 * b:3 * b], o[3 * b:4 * b]).astype(BF16)

    def scores(q4, k):
        return lax.dot_general(q4, k, NT_DIMS, preferred_element_type=F32)

    @pl.when(n < N_WA_BLOCKS)
    def _():
        qrow = pl.multiple_of(CTX_LEN + n * WA_BLOCK, WA_BLOCK)
        prow = pl.multiple_of(qrow - WA_BLOCK, WA_BLOCK)
        nrow = pl.multiple_of(jnp.minimum(qrow + WA_BLOCK, TOK - WA_BLOCK), WA_BLOCK)
        k_blocks = [slab_ref[pl.ds(s, WA_BLOCK), kcol] for s in (prow, qrow, nrow)]
        v_blocks = [slab_ref[pl.ds(s, WA_BLOCK), vcol] for s in (prow, qrow, nrow)]
        i = lax.broadcasted_iota(jnp.int32, (rows4, WA_BLOCK), 0) & (WA_BLOCK - 1)
        jj = lax.broadcasted_iota(jnp.int32, (rows4, WA_BLOCK), 1)
        keep_prev = (jj >= i) & (n > 0)
        keep_next = (jj <= i) & (n < N_WA_BLOCKS - 1)
        q4 = stack_heads(slab_ref[pl.ds(qrow, WA_BLOCK), 0:LANES], slab_ref[pl.ds(qrow, WA_BLOCK), LANES:2 * LANES])
        s_p = jnp.where(keep_prev, scores(q4, k_blocks[0]), NEG)
        s_m = scores(q4, k_blocks[1])
        s_n = jnp.where(keep_next, scores(q4, k_blocks[2]), NEG)
        softmax_pv([s_p, s_m, s_n, scores(q4, kc)], v_blocks + [vc])

    @pl.when(n >= N_WA_BLOCKS)
    def _():
        qrow = pl.multiple_of((n - N_WA_BLOCKS) * WA_BLOCK, WA_BLOCK)
        q4 = stack_heads(slab_ref[pl.ds(qrow, WA_BLOCK), 0:LANES], slab_ref[pl.ds(qrow, WA_BLOCK), LANES:2 * LANES])
        softmax_pv([scores(q4, kc)], [vc])


def _wa_call(wa, sink, with_ctx):
    B = wa.shape[0]
    n_ctx = CTX_LEN // WA_BLOCK if with_ctx else 0
    out_rows = SEQ + n_ctx * WA_BLOCK

    def out_map(b, n):
        if with_ctx:
            return (b, jnp.where(n < N_WA_BLOCKS, n + n_ctx, n - N_WA_BLOCKS), 0)
        return (b, n, 0)

    return pl.pallas_call(
        _wa_kernel,
        grid=(B, N_WA_BLOCKS + n_ctx),
        in_specs=[
            pl.BlockSpec(memory_space=pltpu.SMEM),
            pl.BlockSpec((None, TOK, 512), lambda b, n: (b, 0, 0)),
        ],
        out_specs=pl.BlockSpec((None, WA_BLOCK, WA_Q_WIDTH), out_map),
        out_shape=jax.ShapeDtypeStruct((B, out_rows, WA_Q_WIDTH), BF16),
        compiler_params=_cparams(("arbitrary", "arbitrary")),
        name="wa_attn",
    )(sink, wa)


GLA_LEVELS = int(math.log2(GLA_CHUNK))
GLA_ROW_LEVELS = 3
E_ROWS = (1 + GLA_LEVELS - GLA_ROW_LEVELS) * GLA_CHUNK


def _gla_constants():
    C = GLA_CHUNK
    p = np.arange(C)
    r = p[None, :]
    mats = [r <= p[:, None]]
    masks = []
    for lvl in range(GLA_LEVELS):
        m = C >> (lvl + 1)
        pair = p // (2 * m)
        half = (p // m) % 2
        a_end = pair * 2 * m + m - 1
        if lvl >= GLA_ROW_LEVELS:
            mats.append(r <= a_end[:, None])
        masks.append((pair[:, None] == pair[None, :]) & (half[:, None] == 1) & (half[None, :] == 0))
    masks.append(p[:, None] == p[None, :])
    fwd = np.concatenate(mats, axis=0).astype(np.float32)
    bwd = np.concatenate([blk[::-1, ::-1] for blk in mats], axis=0).astype(np.float32)
    msk = np.stack(masks).astype(np.float32)
    lmask = np.stack([msk, msk[:, ::-1, ::-1]])
    lmask = np.concatenate([lmask, lmask], axis=-1)
    return np.stack([fwd, bwd]), lmask


def _gla_kernel(emat_ref, lmask_ref, qkf_ref, laf_ref, vf_ref, qkb_ref, lab_ref, vb_ref,
                of_ref, ob_ref, st_ref):
    i = pl.program_id(1)

    @pl.when(i == 0)
    def _():
        st_ref[...] = jnp.zeros_like(st_ref)

    C = GLA_CHUNK
    lo = _lane_lo((C, LANES))
    v_lo = _lane_lo((C, 2 * GLA_DV), width=2 * GLA_DV)
    st_rows = lax.broadcasted_iota(jnp.int32, (2 * GLA_DV, LANES), 0) < GLA_DV
    st_keep = st_rows == _lane_lo((2 * GLA_DV, LANES))

    def head_block(kk):
        zero = jnp.zeros_like(kk)
        return jnp.concatenate([jnp.where(lo, kk, zero), jnp.where(lo, zero, kk)], axis=0)

    def chunk(d, p, r0, qk_ref, la_ref, v_ref, o_ref):
        rows = slice(r0, r0 + C)
        g = la_ref[rows, p * LANES:(p + 1) * LANES]
        q = qk_ref[rows, p * LANES:(p + 1) * LANES]
        k = qk_ref[rows, GLA_QK_WIDTH + p * LANES:GLA_QK_WIDTH + (p + 1) * LANES]
        v = v_ref[rows, p * 2 * GLA_DV:(p + 1) * 2 * GLA_DV]
        g_hi = g.astype(BF16)
        g_lo = (g - g_hi.astype(F32)).astype(BF16)
        e2 = jnp.dot(emat_ref[d], jnp.concatenate([g_hi, g_lo], axis=1), preferred_element_type=F32)
        e = e2[:, 0:LANES] + e2[:, LANES:2 * LANES]
        b = e[0:C]
        end_row = C - 1 if d == 0 else 0
        b_end = b[end_row:end_row + 1]
        st = st_ref[d, p]
        o = lax.dot_general((q * jnp.exp(b)).astype(BF16), st.astype(BF16), NT_DIMS,
                            preferred_element_type=F32)
        att = lax.dot_general(q.astype(BF16), head_block(k.astype(BF16)), NT_DIMS,
                              preferred_element_type=F32) * lmask_ref[d, GLA_LEVELS]
        for lvl in range(GLA_LEVELS):
            m = C >> (lvl + 1)
            if lvl < GLA_ROW_LEVELS:
                pieces = []
                for u in range(C // (2 * m)):
                    row = 2 * m * u + (m - 1 if d == 0 else m)
                    pieces.append(jnp.broadcast_to(b[row:row + 1], (2 * m, LANES)))
                rho = pieces[0] if len(pieces) == 1 else jnp.concatenate(pieces, axis=0)
            else:
                rho = e[(lvl - GLA_ROW_LEVELS + 1) * C:(lvl - GLA_ROW_LEVELS + 2) * C]
            x = jnp.exp(-jnp.abs(b - rho))
            att = att + lax.dot_general((q * x).astype(BF16), head_block((k * x).astype(BF16)), NT_DIMS,
                                        preferred_element_type=F32) * lmask_ref[d, lvl]
        vzero = jnp.zeros_like(v)
        v_bd = jnp.concatenate([jnp.where(v_lo, v, vzero), jnp.where(v_lo, vzero, v)], axis=0)
        o = o + jnp.dot(att.astype(BF16), v_bd, preferred_element_type=F32)
        o_ref[rows, p * 2 * GLA_DV:(p + 1) * 2 * GLA_DV] = o.astype(BF16)
        ks = (k * jnp.exp(b_end - b)).astype(BF16)
        upd = lax.dot_general(v, ks, TN_DIMS, preferred_element_type=F32)
        st_ref[d, p] = st * jnp.exp(b_end) + jnp.where(st_keep, upd, 0.0)

    n_chunks = TILE // C
    for step in range(n_chunks):
        for p in range(GLA_HEADS // 2):
            chunk(0, p, step * C, qkf_ref, laf_ref, vf_ref, of_ref)
            chunk(1, p, (n_chunks - 1 - step) * C, qkb_ref, lab_ref, vb_ref, ob_ref)


def _gla_call(gqk, la, gv, emat, lmask):
    B = gqk.shape[0]
    fwd = lambda b, i: (b, i, 0)
    bwd = lambda b, i: (b, jnp.where(i == 0, 0, N_TILES - i), 0)
    bwd_la = lambda b, i: (b, jnp.where(i == 0, 0, N_TILES - i), 1)
    return pl.pallas_call(
        _gla_kernel,
        grid=(B, N_TILES),
        in_specs=[
            pl.BlockSpec((2, E_ROWS, GLA_CHUNK), lambda b, i: (0, 0, 0)),
            pl.BlockSpec((2, GLA_LEVELS + 1, GLA_CHUNK, LANES), lambda b, i: (0, 0, 0, 0)),
            pl.BlockSpec((None, TILE, 512), fwd),
            pl.BlockSpec((None, TILE, 256), fwd),
            pl.BlockSpec((None, TILE, 512), fwd),
            pl.BlockSpec((None, TILE, 512), bwd),
            pl.BlockSpec((None, TILE, 256), bwd_la),
            pl.BlockSpec((None, TILE, 512), bwd),
        ],
        out_specs=[pl.BlockSpec((None, TILE, 512), fwd), pl.BlockSpec((None, TILE, 512), bwd)],
        out_shape=[jax.ShapeDtypeStruct((B, TOK, 512), BF16)] * 2,
        scratch_shapes=[pltpu.VMEM((2, GLA_HEADS // 2, 2 * GLA_DV, LANES), F32)],
        compiler_params=_cparams(("arbitrary", "arbitrary")),
        name="gla_scan",
    )(emat, lmask, gqk, la, gv, gqk, la, gv)


def _outproj_kernel(xc_ref, xl_ref, mod_ref, na_ref, wa_ref, of_ref, ob_ref, go_ref, gn_ref, w_ref, o_ref,
                    *, two_source):
    x = _pick_tile(xc_ref, xl_ref) if two_source else xl_ref[...]
    og = of_ref[...].astype(F32) + ob_ref[...].astype(F32)
    gn = gn_ref[...]
    parts = []
    for h in range(GLA_HEADS):
        seg = og[:, h * GLA_DV:(h + 1) * GLA_DV]
        ms = jnp.mean(seg * seg, axis=-1, keepdims=True)
        parts.append(seg * lax.rsqrt(ms + EPS) * gn)
    go = go_ref[...].astype(F32)
    y_g = (jnp.concatenate(parts, axis=1) * (go * jax.nn.sigmoid(go))).astype(BF16)
    res = (jnp.dot(na_ref[...], w_ref[0:256, :], preferred_element_type=F32)
           + jnp.dot(wa_ref[...], w_ref[256:512, :], preferred_element_type=F32)
           + jnp.dot(y_g, w_ref[512:1024, :], preferred_element_type=F32))
    gate = mod_ref[...][:, 2 * D_MODEL:3 * D_MODEL]
    o_ref[...] = x + gate * res


def _outproj_call(x_first, x_rest, rest_tile0, mod_rows, o_na, o_wa, o_f, o_b, go, gn, w_out, with_ctx):
    B = x_first.shape[0]
    t0 = 0 if with_ctx else 1
    n_t = N_TILES - t0
    full = lambda w: pl.BlockSpec((None, TILE, w), lambda b, i: (b, i + t0, 0))
    own = lambda w: pl.BlockSpec((None, TILE, w), lambda b, i: (b, i, 0))
    if with_ctx:
        spec_c, spec_l = _two_source_specs(x_first, x_rest, rest_tile0)
    else:
        spec_c = spec_l = full(D_MODEL)
    return pl.pallas_call(
        functools.partial(_outproj_kernel, two_source=with_ctx),
        grid=(B, n_t),
        in_specs=[
            spec_c, spec_l,
            pl.BlockSpec((None, 1, 6 * D_MODEL), lambda b, i: (2 * b + jnp.minimum(i + t0, 1), 0, 0)),
            own(NA_WIDTH), own(WA_Q_WIDTH), full(512), full(512), full(512),
            pl.BlockSpec((1, GLA_DV), lambda b, i: (0, 0)),
            pl.BlockSpec((MIX_WIDTH, D_MODEL), lambda b, i: (0, 0)),
        ],
        out_specs=own(D_MODEL),
        out_shape=jax.ShapeDtypeStruct((B, n_t * TILE, D_MODEL), F32),
        compiler_params=_cparams(("arbitrary", "arbitrary")),
        name="out_proj",
    )(x_first, x_rest, mod_rows, o_na, o_wa, o_f, o_b, go, gn, w_out)


HALO = SUBLANES


def _ffn_kernel(x_ref, xp_ref, xn_ref, mod_ref, g2_ref, wv_ref, wg_ref, cw_ref, cb_ref, wd_ref,
                fg_ref, o_ref, gs_ref, *, t0, final):
    ti = pl.program_id(1) + t0
    mod = mod_ref[...]
    shift, scale = mod[:, 3 * D_MODEL:4 * D_MODEL], mod[:, 4 * D_MODEL:5 * D_MODEL]
    g2 = g2_ref[...]
    x = x_ref[...]
    h_t = _mod_norm(x, g2, shift, scale)
    h_ext = jnp.concatenate([_mod_norm(xp_ref[...], g2, shift, scale), h_t,
                             _mod_norm(xn_ref[...], g2, shift, scale)], axis=0).astype(BF16)
    h_b = h_t.astype(BF16)
    seq_start = ti <= 1
    seq_end = (ti == 0) | (ti == N_TILES - 1)
    acc = jnp.zeros((TILE, D_MODEL), F32)
    for c in range(FF_DIM // FF_CHUNK):
        cols = slice(c * FF_CHUNK, (c + 1) * FF_CHUNK)
        g_ext = jnp.dot(h_ext, wg_ref[:, cols], preferred_element_type=F32)
        gs_ref[...] = g_ext
        gs_ref[0:HALO, :] = jnp.where(seq_start, 0.0, g_ext[0:HALO])
        gs_ref[HALO + TILE:, :] = jnp.where(seq_end, 0.0, g_ext[HALO + TILE:])
        cw = cw_ref[:, cols]
        gate = (cw[0:1] * gs_ref[pl.ds(HALO - 1, TILE), :] + cw[1:2] * gs_ref[pl.ds(HALO, TILE), :]
                + cw[2:3] * gs_ref[pl.ds(HALO + 1, TILE), :] + cb_ref[:, cols])
        val = jnp.dot(h_b, wv_ref[:, cols], preferred_element_type=F32)
        act = 0.5 * gate * (1.0 + lax.erf(gate * (1.0 / math.sqrt(2.0))))
        acc = acc + jnp.dot((act * val).astype(BF16), wd_ref[cols, :], preferred_element_type=F32)
    y = x + mod[:, 5 * D_MODEL:6 * D_MODEL] * acc
    if final:
        ms = jnp.mean(y * y, axis=-1, keepdims=True)
        y = y * lax.rsqrt(ms + EPS) * fg_ref[...]
    o_ref[...] = y


def _ffn_call(x1, mod_rows, g2, w_up, cw, cb, wd, fg, with_ctx, final):
    B = x1.shape[0]
    t0 = 0 if with_ctx else 1
    n_t = x1.shape[1] // TILE
    per8 = TILE // HALO
    last8 = x1.shape[1] // HALO - 1
    const = lambda r, c: pl.BlockSpec((r, c), lambda b, i: (0, 0))
    return pl.pallas_call(
        functools.partial(_ffn_kernel, t0=t0, final=final),
        grid=(B, n_t),
        in_specs=[
            pl.BlockSpec((None, TILE, D_MODEL), lambda b, i: (b, i, 0)),
            pl.BlockSpec((None, HALO, D_MODEL), lambda b, i: (b, jnp.maximum(i * per8 - 1, 0), 0)),
            pl.BlockSpec((None, HALO, D_MODEL), lambda b, i: (b, jnp.minimum((i + 1) * per8, last8), 0)),
            pl.BlockSpec((None, 1, 6 * D_MODEL), lambda b, i: (2 * b + jnp.minimum(i + t0, 1), 0, 0)),
            const(1, D_MODEL),
            pl.BlockSpec((D_MODEL, FF_DIM), lambda b, i: (0, 0)),
            pl.BlockSpec((D_MODEL, FF_DIM), lambda b, i: (0, 1)),
            const(3, FF_DIM), const(1, FF_DIM),
            const(FF_DIM, D_MODEL),
            const(1, D_MODEL),
        ],
        out_specs=pl.BlockSpec((None, TILE, D_MODEL), lambda b, i: (b, i, 0)),
        out_shape=jax.ShapeDtypeStruct(x1.shape, F32),
        scratch_shapes=[pltpu.VMEM((TILE + 2 * HALO, FF_CHUNK), F32)],
        compiler_params=_cparams(("arbitrary", "arbitrary")),
        name="conv_ffn",
    )(x1, x1, x1, mod_rows, g2, w_up, w_up, cw, cb, wd, fg)


def _rope_tables():
    t = np.arange(SEQ)
    n_freq = WA_HEAD_DIM // 4
    inv = ROPE_THETA ** (-np.arange(n_freq, dtype=np.float32) / n_freq)
    ang = np.concatenate([(t // GRID_W).astype(np.float32)[:, None] * inv,
                          (t % GRID_W).astype(np.float32)[:, None] * inv], axis=-1)
    cos = np.tile(np.cos(ang), (1, 2 * WA_HEADS))
    sin = np.tile(np.sin(ang), (1, 2 * WA_HEADS))
    cos = np.concatenate([np.ones((CTX_LEN, 256), np.float32), cos], axis=0)
    sin = np.concatenate([np.zeros((CTX_LEN, 256), np.float32), sin], axis=0)
    return jnp.asarray(cos, F32), jnp.asarray(sin, F32)


def _rotate_half_cols(w, n_heads):
    w = w.reshape(w.shape[0], n_heads, 2, WA_HEAD_DIM // 2)
    return jnp.stack([-w[:, :, 1], w[:, :, 0]], axis=2).reshape(w.shape[0], n_heads * WA_HEAD_DIM)


def _layer_weights(w_in, gate_w, gate_b, w_out):
    waq = w_in[:, 3 * NA_WIDTH:3 * NA_WIDTH + WA_Q_WIDTH].reshape(D_MODEL, WA_HEADS, WA_HEAD_DIM)
    waq = waq[:, np.array(WA_HEAD_ORDER)].reshape(D_MODEL, WA_Q_WIDTH)
    wak = w_in[:, W_WAK:W_WAV]
    lr_pad = jnp.zeros((D_MODEL, X_COLS - X_LR - 2 * GLA_GATE_RANK), w_in.dtype)
    w_extra = jnp.concatenate([waq, _rotate_half_cols(waq, WA_HEADS), _rotate_half_cols(wak, WA_KV_HEADS),
                               w_in[:, W_LR:], lr_pad], axis=1).astype(BF16)
    zero = jnp.zeros((GLA_GATE_RANK, GLA_QK_WIDTH), F32)
    gw = jnp.concatenate([
        jnp.concatenate([gate_w[0], zero], axis=1),
        jnp.concatenate([zero, gate_w[1]], axis=1),
        jnp.zeros((LANES - 2 * GLA_GATE_RANK, 2 * GLA_QK_WIDTH), F32)], axis=0).astype(BF16)
    gb = gate_b.reshape(1, 2 * GLA_QK_WIDTH)
    wo_wa = w_out[NA_WIDTH:NA_WIDTH + WA_Q_WIDTH].reshape(WA_HEADS, WA_HEAD_DIM, D_MODEL)[np.array(WA_HEAD_ORDER)]
    wo = jnp.concatenate([w_out[:NA_WIDTH], wo_wa.reshape(WA_Q_WIDTH, D_MODEL),
                          w_out[NA_WIDTH + WA_Q_WIDTH:]], axis=0).astype(BF16)
    return w_in.astype(BF16), w_extra, gw, gb, wo


def kernel(x, c, ctx, c_ctx, w_mod, b_mod, norm1_g, norm2_g, w_in, na_rpb, wa_sink, gla_gate_w,
           gla_gate_b, gla_norm_g, w_out, ffn_w_up, ffn_conv_w, ffn_conv_b, ffn_w_down, final_norm_g):
    B = x.shape[0]
    assert x.shape == (B, SEQ, D_MODEL) and ctx.shape == (B, CTX_LEN, D_MODEL)
    n_c = -(-(B + 1) // SUBLANES) * SUBLANES
    cc = jnp.concatenate([c, c_ctx[None], jnp.zeros((n_c - B - 1, D_MODEL), F32)], axis=0)
    mods = _mod_call(cc, w_mod, b_mod)
    cos_t, sin_t = _rope_tables()
    emat, lmask = _gla_constants()
    emat = jnp.asarray(emat, BF16)
    lmask = jnp.asarray(lmask, F32)
    fg = final_norm_g.reshape(1, D_MODEL)

    x_first, x_rest, rest_tile0 = ctx, x, 0
    out = None
    for l in range(DEPTH):
        last = l == DEPTH - 1
        mod_rows = jnp.stack([jnp.broadcast_to(mods[l, B], (B, 6 * D_MODEL)), mods[l, :B]],
                             axis=1).reshape(2 * B, 1, 6 * D_MODEL)
        w_main, w_extra, gw, gb, wo = _layer_weights(w_in[l], gla_gate_w[l], gla_gate_b[l], w_out[l])
        na, wa, gqk, gv, go, la = _inproj_call(x_first, x_rest, rest_tile0, mod_rows,
                                               norm1_g[l].reshape(1, D_MODEL), w_main, w_extra, gw, gb, cos_t, sin_t)
        bias = _nabias_call(na_rpb[l])
        o_na = _na_call(na, bias, with_ctx=not last)
        o_wa = _wa_call(wa, wa_sink[l], with_ctx=not last)
        o_f, o_b = _gla_call(gqk, la, gv, emat, lmask)
        x1 = _outproj_call(x_first, x_rest, rest_tile0, mod_rows, o_na, o_wa, o_f, o_b, go,
                           gla_norm_g[l].reshape(1, GLA_DV), wo, with_ctx=not last)
        x2 = _ffn_call(x1, mod_rows, norm2_g[l].reshape(1, D_MODEL), ffn_w_up[l].astype(BF16),
                       ffn_conv_w[l], ffn_conv_b[l].reshape(1, FF_DIM), ffn_w_down[l].astype(BF16), fg,
                       with_ctx=not last, final=last)
        if last:
            out = x2
        else:
            x_first, x_rest, rest_tile0 = x2, x2, 1
    return out
```

```python
import functools
import math

import numpy as np
import jax
import jax.numpy as jnp
from jax import lax
from jax.experimental import pallas as pl
from jax.experimental.pallas import tpu as pltpu

D_MODEL = 1024
SEQ = 2048
DEPTH = 2
CTX_LEN = 256
GRID_W = 64
GRID_ROWS = SEQ // GRID_W
TOK = CTX_LEN + SEQ

NA_HEADS = 4
NA_HEAD_DIM = 64
NA_WIN_R = 8
NA_WIN_C = 16
WA_HEADS = 4
WA_KV_HEADS = 2
WA_HEAD_DIM = 64
WA_WINDOW = 128
WA_BLOCK = 128
GLA_HEADS = 4
GLA_DK = 64
GLA_DV = 128
GLA_GATE_RANK = 16
GLA_GATE_TAU = 16.0
GLA_CHUNK = 64

NA_WIDTH = NA_HEADS * NA_HEAD_DIM
WA_Q_WIDTH = WA_HEADS * WA_HEAD_DIM
WA_KV_WIDTH = WA_KV_HEADS * WA_HEAD_DIM
GLA_QK_WIDTH = GLA_HEADS * GLA_DK
GLA_V_WIDTH = GLA_HEADS * GLA_DV
MIX_WIDTH = NA_WIDTH + WA_Q_WIDTH + GLA_V_WIDTH
IN_WIDTH = 3 * NA_WIDTH + WA_Q_WIDTH + 2 * WA_KV_WIDTH + 2 * GLA_QK_WIDTH + 2 * GLA_V_WIDTH + 2 * GLA_GATE_RANK

FF_DIM = 2816
FF_CHUNK = 1408
ROPE_THETA = 10000.0
EPS = 1e-6

BP = 2
TILE = 256
N_TILES = TOK // TILE
LANES = 128
SUBLANES = 8
NEG = -1e30
VMEM_LIMIT = 56 * 1024 * 1024

F32 = jnp.float32
BF16 = jnp.bfloat16
NT_DIMS = (((1,), (1,)), ((), ()))
TN_DIMS = (((0,), (0,)), ((), ()))

W_NA = 0
W_WAK = 1024
W_WAV = 1152
W_GQ = 1280
W_GK = 1536
W_GV = 1792
W_GO = 2304
W_LR = 2816
X_WAQ = 0
X_WAQS = 256
X_WAKS = 512
X_LR = 640
X_COLS = 768

WA_HEAD_ORDER = (0, 2, 1, 3)
Q_SCALE = NA_HEAD_DIM ** -0.5


def _cparams(sem):
    return pltpu.CompilerParams(dimension_semantics=sem, vmem_limit_bytes=VMEM_LIMIT)


def _lane_lo(shape, width=LANES):
    lane = lax.broadcasted_iota(jnp.int32, shape, len(shape) - 1)
    return (lane & (width - 1)) < (width // 2)


def _mod_kernel(c_ref, w_ref, b_ref, o_ref):
    c = c_ref[...]
    sc = c * jax.nn.sigmoid(c)
    o_ref[...] = jnp.dot(sc.astype(BF16), w_ref[...].astype(BF16),
                         preferred_element_type=F32) + b_ref[...]


def _mod_call(cc, w_mod, b_mod):
    tn = 1536
    rows = cc.shape[0]
    return pl.pallas_call(
        _mod_kernel,
        grid=(DEPTH, 6 * D_MODEL // tn),
        in_specs=[
            pl.BlockSpec((rows, D_MODEL), lambda l, j: (0, 0)),
            pl.BlockSpec((None, D_MODEL, tn), lambda l, j: (l, 0, j)),
            pl.BlockSpec((None, 1, tn), lambda l, j: (l, 0, j)),
        ],
        out_specs=pl.BlockSpec((None, rows, tn), lambda l, j: (l, 0, j)),
        out_shape=jax.ShapeDtypeStruct((DEPTH, rows, 6 * D_MODEL), F32),
        compiler_params=_cparams(("arbitrary", "arbitrary")),
        name="adaln_mod",
    )(cc, w_mod, b_mod.reshape(DEPTH, 1, 6 * D_MODEL))


N_RO = 2 * NA_WIN_R - 1
N_CO = 2 * NA_WIN_C - 1
N_BIAS = N_RO + 1


def _nabias_kernel(rpb_ref, o_ref):
    h = pl.program_id(0)
    shape = (GRID_W, LANES)
    q = lax.broadcasted_iota(jnp.int32, shape, 0)
    kc = lax.broadcasted_iota(jnp.int32, shape, 1) & (GRID_W - 1)
    d = kc - q + (NA_WIN_C - 1)
    cs = jnp.clip(q - NA_WIN_C // 2, 0, GRID_W - NA_WIN_C)
    inwin = (kc >= cs) & (kc < cs + NA_WIN_C)
    base = h * (N_RO * N_CO)
    o_ref[0] = jnp.full(shape, NEG, F32)
    for ro in range(N_RO):
        acc = jnp.full(shape, NEG, F32)
        for dd in range(N_CO):
            acc = jnp.where(d == dd, rpb_ref[base + ro * N_CO + dd], acc)
        o_ref[1 + ro] = jnp.where(inwin, acc, NEG)


def _nabias_call(rpb):
    return pl.pallas_call(
        _nabias_kernel,
        grid=(NA_HEADS,),
        in_specs=[pl.BlockSpec(memory_space=pltpu.SMEM)],
        out_specs=pl.BlockSpec((None, N_BIAS, GRID_W, LANES), lambda h: (h, 0, 0, 0)),
        out_shape=jax.ShapeDtypeStruct((NA_HEADS, N_BIAS, GRID_W, LANES), F32),
        compiler_params=_cparams(("arbitrary",)),
        name="na_bias",
    )(rpb.reshape(-1))


def _mod_norm(x, g, shift, scale):
    ms = jnp.mean(x * x, axis=-1, keepdims=True)
    return (x * lax.rsqrt(ms + EPS) * g) * (1.0 + scale) + shift


def _pick_tile(first_ref, rest_ref):
    return jnp.where(pl.program_id(1) == 0, first_ref[...], rest_ref[...])


def _rows2d(a):
    return a.reshape(a.shape[0] * a.shape[1], a.shape[2])


def _inproj_kernel(xc_ref, xl_ref, mod_ref, g1_ref, w_ref, wx_ref, gw_ref, gb_ref, cos_ref, sin_ref,
                   na_ref, wa_ref, gqk_ref, gv_ref, go_ref, la_ref):
    mod = mod_ref[...]
    h = _mod_norm(_pick_tile(xc_ref, xl_ref), g1_ref[...], mod[:, :, 0:D_MODEL], mod[:, :, D_MODEL:2 * D_MODEL])
    hb = _rows2d(h).astype(BF16)

    def proj(w, c0, c1):
        return jnp.dot(hb, w[:, c0:c1], preferred_element_type=F32).reshape(BP, TILE, c1 - c0)

    na_ref[:, :, 0:NA_WIDTH] = (proj(w_ref, W_NA, W_NA + NA_WIDTH) * Q_SCALE).astype(BF16)
    na_ref[:, :, NA_WIDTH:3 * NA_WIDTH] = proj(w_ref, W_NA + NA_WIDTH, W_NA + 3 * NA_WIDTH).astype(BF16)
    cos = cos_ref[...]
    sin = sin_ref[...]
    wa_ref[:, :, 0:256] = ((proj(wx_ref, X_WAQ, X_WAQS) * cos + proj(wx_ref, X_WAQS, X_WAKS) * sin)
                           * Q_SCALE).astype(BF16)
    wa_ref[:, :, 256:384] = (proj(w_ref, W_WAK, W_WAV) * cos[:, 0:LANES]
                             + proj(wx_ref, X_WAKS, X_LR) * sin[:, 0:LANES]).astype(BF16)
    wa_ref[:, :, 384:512] = proj(w_ref, W_WAV, W_GQ).astype(BF16)
    gqk_ref[:, :, 0:GLA_QK_WIDTH] = proj(w_ref, W_GQ, W_GK) * Q_SCALE
    gqk_ref[:, :, GLA_QK_WIDTH:] = proj(w_ref, W_GK, W_GV)
    gv_ref[...] = proj(w_ref, W_GV, W_GO).astype(BF16)
    go_ref[...] = proj(w_ref, W_GO, W_LR).astype(BF16)
    lr = _rows2d(proj(wx_ref, X_LR, X_COLS)).astype(BF16)
    logit = jnp.dot(lr, gw_ref[...], preferred_element_type=F32) + gb_ref[...]
    log_sig = jnp.minimum(logit, 0.0) - jnp.log1p(jnp.exp(-jnp.abs(logit)))
    la_ref[...] = (log_sig * (1.0 / GLA_GATE_TAU)).reshape(BP, TILE, 2 * GLA_QK_WIDTH)


def _two_source_specs(rest_tile0):
    return (pl.BlockSpec((BP, TILE, D_MODEL), lambda b, i: (b, 0, 0)),
            pl.BlockSpec((BP, TILE, D_MODEL), lambda b, i: (b, jnp.maximum(i - 1, 0) + rest_tile0, 0)))


def _mod_spec(t0):
    return pl.BlockSpec((None, BP, 1, 6 * D_MODEL), lambda b, i: (jnp.minimum(i + t0, 1), b, 0, 0))


def _inproj_call(x_first, x_rest, rest_tile0, mod_rows, g1, w_main, w_extra, gw, gb, cos_t, sin_t):
    B = x_first.shape[0]
    tile = lambda w: pl.BlockSpec((BP, TILE, w), lambda b, i: (b, i, 0))
    const = lambda r, c: pl.BlockSpec((r, c), lambda b, i: (0, 0))
    spec_c, spec_l = _two_source_specs(rest_tile0)
    return pl.pallas_call(
        _inproj_kernel,
        grid=(B // BP, N_TILES),
        in_specs=[
            spec_c, spec_l,
            _mod_spec(0),
            const(1, D_MODEL),
            const(D_MODEL, IN_WIDTH),
            const(D_MODEL, X_COLS),
            const(LANES, 2 * GLA_QK_WIDTH),
            const(1, 2 * GLA_QK_WIDTH),
            pl.BlockSpec((TILE, 256), lambda b, i: (i, 0)),
            pl.BlockSpec((TILE, 256), lambda b, i: (i, 0)),
        ],
        out_specs=[tile(768), tile(512), tile(512), tile(512), tile(512), tile(512)],
        out_shape=[
            jax.ShapeDtypeStruct((B, TOK, 768), BF16),
            jax.ShapeDtypeStruct((B, TOK, 512), BF16),
            jax.ShapeDtypeStruct((B, TOK, 512), F32),
            jax.ShapeDtypeStruct((B, TOK, 512), BF16),
            jax.ShapeDtypeStruct((B, TOK, 512), BF16),
            jax.ShapeDtypeStruct((B, TOK, 512), F32),
        ],
        compiler_params=_cparams(("arbitrary", "arbitrary")),
        name="in_proj",
    )(x_first, x_rest, mod_rows, g1, w_main, w_extra, gw, gb, cos_t, sin_t)


NA_ROWS_PER_STEP = TILE // GRID_W
NA_STEPS = GRID_ROWS // NA_ROWS_PER_STEP
NA_UNION = NA_WIN_R + NA_ROWS_PER_STEP


def _split_heads_rows(q):
    lo = _lane_lo(q.shape)
    zero = jnp.zeros_like(q)
    return jnp.concatenate([jnp.where(lo, q, zero), jnp.where(lo, zero, q)], axis=0)


def _merge_heads_rows(o, n):
    return jnp.where(_lane_lo((n, LANES)), o[0:n], o[n:2 * n])


def _na_kernel(slab_ref, bias_ref, o_ref):
    j = pl.program_id(1)
    n_union = NA_UNION * GRID_W

    def ctx_kv(p):
        return (slab_ref[0:CTX_LEN, 256 + p * LANES:256 + (p + 1) * LANES],
                slab_ref[0:CTX_LEN, 512 + p * LANES:512 + (p + 1) * LANES])

    @pl.when(j < NA_STEPS)
    def _():
        u0 = jnp.clip(j * NA_ROWS_PER_STEP - NA_WIN_R // 2, 0, GRID_ROWS - NA_UNION)
        qrow = pl.multiple_of(CTX_LEN + j * TILE, TILE)
        krow = pl.multiple_of(CTX_LEN + u0 * GRID_W, GRID_W)
        lo = _lane_lo((GRID_W, LANES))
        tile_idx = []
        for a in range(NA_ROWS_PER_STEP):
            r = j * NA_ROWS_PER_STEP + a
            rs = jnp.clip(r - NA_WIN_R // 2, 0, GRID_ROWS - NA_WIN_R)
            idx = []
            for u in range(NA_UNION):
                key_row = u0 + u
                inside = (key_row >= rs) & (key_row < rs + NA_WIN_R)
                idx.append(jnp.where(inside, key_row - r + NA_WIN_R, 0))
            tile_idx.append(idx)
        for p in range(NA_HEADS // 2):
            kc, vc = ctx_kv(p)
            ku = slab_ref[pl.ds(krow, n_union), 256 + p * LANES:256 + (p + 1) * LANES]
            vu = slab_ref[pl.ds(krow, n_union), 512 + p * LANES:512 + (p + 1) * LANES]
            q_all = slab_ref[pl.ds(qrow, TILE), p * LANES:(p + 1) * LANES]
            q2 = jnp.concatenate([_split_heads_rows(q_all[a * GRID_W:(a + 1) * GRID_W])
                                  for a in range(NA_ROWS_PER_STEP)], axis=0)
            bias = jnp.concatenate([
                jnp.concatenate([jnp.where(lo, bias_ref[2 * p + hh, tile_idx[a][2 * jj]],
                                           bias_ref[2 * p + hh, tile_idx[a][2 * jj + 1]])
                                 for jj in range(NA_UNION // 2)], axis=1)
                for a in range(NA_ROWS_PER_STEP) for hh in range(2)], axis=0)
            s_u = lax.dot_general(q2, ku, NT_DIMS, preferred_element_type=F32) + bias
            s_c = lax.dot_general(q2, kc, NT_DIMS, preferred_element_type=F32)
            m = jnp.maximum(jnp.max(s_u, axis=-1, keepdims=True), jnp.max(s_c, axis=-1, keepdims=True))
            p_u = jnp.exp(s_u - m)
            p_c = jnp.exp(s_c - m)
            den = jnp.sum(p_u, axis=-1, keepdims=True) + jnp.sum(p_c, axis=-1, keepdims=True)
            o = (jnp.dot(p_u.astype(BF16), vu, preferred_element_type=F32)
                 + jnp.dot(p_c.astype(BF16), vc, preferred_element_type=F32)) / den
            out = [_merge_heads_rows(o[a * 2 * GRID_W:(a + 1) * 2 * GRID_W], GRID_W)
                   for a in range(NA_ROWS_PER_STEP)]
            o_ref[:, p * LANES:(p + 1) * LANES] = jnp.concatenate(out, axis=0).astype(BF16)

    @pl.when(j >= NA_STEPS)
    def _():
        for p in range(NA_HEADS // 2):
            kc, vc = ctx_kv(p)
            q2 = _split_heads_rows(slab_ref[0:CTX_LEN, p * LANES:(p + 1) * LANES])
            s_c = lax.dot_general(q2, kc, NT_DIMS, preferred_element_type=F32)
            p_c = jnp.exp(s_c - jnp.max(s_c, axis=-1, keepdims=True))
            den = jnp.sum(p_c, axis=-1, keepdims=True)
            o = jnp.dot(p_c.astype(BF16), vc, preferred_element_type=F32) / den
            o_ref[:, p * LANES:(p + 1) * LANES] = _merge_heads_rows(o, CTX_LEN).astype(BF16)


def _na_call(na, bias, with_ctx):
    B = na.shape[0]
    n_ctx = 1 if with_ctx else 0

    def out_map(b, j):
        if with_ctx:
            return (b, jnp.where(j < NA_STEPS, j + 1, 0), 0)
        return (b, j, 0)

    return pl.pallas_call(
        _na_kernel,
        grid=(B, NA_STEPS + n_ctx),
        in_specs=[
            pl.BlockSpec((None, TOK, 768), lambda b, j: (b, 0, 0)),
            pl.BlockSpec((NA_HEADS, N_BIAS, GRID_W, LANES), lambda b, j: (0, 0, 0, 0)),
        ],
        out_specs=pl.BlockSpec((None, TILE, NA_WIDTH), out_map),
        out_shape=jax.ShapeDtypeStruct((B, SEQ + n_ctx * CTX_LEN, NA_WIDTH), BF16),
        compiler_params=_cparams(("arbitrary", "arbitrary")),
        name="na_attn",
    )(na, bias)


N_WA_BLOCKS = SEQ // WA_BLOCK


def _wa_kernel(sink_ref, slab_ref, o_ref):
    n = pl.program_id(1)
    kcol = slice(256, 384)
    vcol = slice(384, 512)
    kc = slab_ref[0:CTX_LEN, kcol]
    vc = slab_ref[0:CTX_LEN, vcol]
    lo = _lane_lo((WA_BLOCK, LANES))
    rows4 = WA_HEADS * WA_BLOCK
    slot = lax.broadcasted_iota(jnp.int32, (rows4, 1), 0) // WA_BLOCK
    sink = jnp.zeros((rows4, 1), F32)
    for s_i, h in enumerate(WA_HEAD_ORDER):
        sink = jnp.where(slot == s_i, sink_ref[h], sink)

    def stack_heads(qa, qb):
        zero = jnp.zeros_like(qa)
        return jnp.concatenate([jnp.where(lo, qa, zero), jnp.where(lo, zero, qa),
                                jnp.where(lo, qb, zero), jnp.where(lo, zero, qb)], axis=0)

    def softmax_pv(scores, values):
        m = sink
        for s in scores:
            m = jnp.maximum(m, jnp.max(s, axis=-1, keepdims=True))
        den = jnp.exp(sink - m)
        o = None
        for s, v in zip(scores, values):
            e = jnp.exp(s - m)
            den = den + jnp.sum(e, axis=-1, keepdims=True)
            t = jnp.dot(e.astype(BF16), v, preferred_element_type=F32)
            o = t if o is None else o + t
        o = o / den
        b = WA_BLOCK
        o_ref[:, 0:LANES] = jnp.where(lo, o[0:b], o[b:2 * b]).astype(BF16)
        o_ref[:, LANES:2 * LANES] = jnp.where(lo, o[2 * b:3 * b], o[3 * b:4 * b]).astype(BF16)

    def scores(q4, k):
        return lax.dot_general(q4, k, NT_DIMS, preferred_element_type=F32)

    @pl.when(n < N_WA_BLOCKS)
    def _():
        qrow = pl.multiple_of(CTX_LEN + n * WA_BLOCK, WA_BLOCK)
        prow = pl.multiple_of(qrow - WA_BLOCK, WA_BLOCK)
        nrow = pl.multiple_of(jnp.minimum(qrow + WA_BLOCK, TOK - WA_BLOCK), WA_BLOCK)
        k_blocks = [slab_ref[pl.ds(s, WA_BLOCK), kcol] for s in (prow, qrow, nrow)]
        v_blocks = [slab_ref[pl.ds(s, WA_BLOCK), vcol] for s in (prow, qrow, nrow)]
        i = lax.broadcasted_iota(jnp.int32, (rows4, WA_BLOCK), 0) & (WA_BLOCK - 1)
        jj = lax.broadcasted_iota(jnp.int32, (rows4, WA_BLOCK), 1)
        keep_prev = (jj >= i) & (n > 0)
        keep_next = (jj <= i) & (n < N_WA_BLOCKS - 1)
        q4 = stack_heads(slab_ref[pl.ds(qrow, WA_BLOCK), 0:LANES], slab_ref[pl.ds(qrow, WA_BLOCK), LANES:2 * LANES])
        s_p = jnp.where(keep_prev, scores(q4, k_blocks[0]), NEG)
        s_m = scores(q4, k_blocks[1])
        s_n = jnp.where(keep_next, scores(q4, k_blocks[2]), NEG)
        softmax_pv([s_p, s_m, s_n, scores(q4, kc)], v_blocks + [vc])

    @pl.when(n >= N_WA_BLOCKS)
    def _():
        qrow = pl.multiple_of((n - N_WA_BLOCKS) * WA_BLOCK, WA_BLOCK)
        q4 = stack_heads(slab_ref[pl.ds(qrow, WA_BLOCK), 0:LANES], slab_ref[pl.ds(qrow, WA_BLOCK), LANES:2 * LANES])
        softmax_pv([scores(q4, kc)], [vc])


def _wa_call(wa, sink, with_ctx):
    B = wa.shape[0]
    n_ctx = CTX_LEN // WA_BLOCK if with_ctx else 0
    out_rows = SEQ + n_ctx * WA_BLOCK

    def out_map(b, n):
        if with_ctx:
            return (b, jnp.where(n < N_WA_BLOCKS, n + n_ctx, n - N_WA_BLOCKS), 0)
        return (b, n, 0)

    return pl.pallas_call(
        _wa_kernel,
        grid=(B, N_WA_BLOCKS + n_ctx),
        in_specs=[
            pl.BlockSpec(memory_space=pltpu.SMEM),
            pl.BlockSpec((None, TOK, 512), lambda b, n: (b, 0, 0)),
        ],
        out_specs=pl.BlockSpec((None, WA_BLOCK, WA_Q_WIDTH), out_map),
        out_shape=jax.ShapeDtypeStruct((B, out_rows, WA_Q_WIDTH), BF16),
        compiler_params=_cparams(("arbitrary", "arbitrary")),
        name="wa_attn",
    )(sink, wa)


GLA_LEVELS = int(math.log2(GLA_CHUNK))
GLA_ROW_LEVELS = 3
E_ROWS = (1 + GLA_LEVELS - GLA_ROW_LEVELS) * GLA_CHUNK


def _gla_constants():
    C = GLA_CHUNK
    p = np.arange(C)
    r = p[None, :]
    mats = [r <= p[:, None]]
    masks = []
    for lvl in range(GLA_LEVELS):
        m = C >> (lvl + 1)
        pair = p // (2 * m)
        half = (p // m) % 2
        a_end = pair * 2 * m + m - 1
        if lvl >= GLA_ROW_LEVELS:
            mats.append(r <= a_end[:, None])
        masks.append((pair[:, None] == pair[None, :]) & (half[:, None] == 1) & (half[None, :] == 0))
    masks.append(p[:, None] == p[None, :])
    fwd = np.concatenate(mats, axis=0).astype(np.float32)
    bwd = np.concatenate([blk[::-1, ::-1] for blk in mats], axis=0).astype(np.float32)
    msk = np.stack(masks).astype(np.float32)
    lmask = np.stack([msk, msk[:, ::-1, ::-1]])
    lmask = np.concatenate([lmask, lmask], axis=-1)
    return np.stack([fwd, bwd]), lmask


def _gla_kernel(emat_ref, lmask_ref, qkf_ref, laf_ref, vf_ref, qkb_ref, lab_ref, vb_ref,
                of_ref, ob_ref, st_ref):
    i = pl.program_id(1)

    @pl.when(i == 0)
    def _():
        st_ref[...] = jnp.zeros_like(st_ref)

    C = GLA_CHUNK
    lo = _lane_lo((C, LANES))
    v_lo = _lane_lo((C, 2 * GLA_DV), width=2 * GLA_DV)
    st_rows = lax.broadcasted_iota(jnp.int32, (2 * GLA_DV, LANES), 0) < GLA_DV
    st_keep = st_rows == _lane_lo((2 * GLA_DV, LANES))

    def head_block(kk):
        zero = jnp.zeros_like(kk)
        return jnp.concatenate([jnp.where(lo, kk, zero), jnp.where(lo, zero, kk)], axis=0)

    def chunk(bb, d, p, r0, qk_ref, la_ref, v_ref, o_ref):
        rows = slice(r0, r0 + C)
        g = la_ref[bb, rows, p * LANES:(p + 1) * LANES]
        q = qk_ref[bb, rows, p * LANES:(p + 1) * LANES]
        k = qk_ref[bb, rows, GLA_QK_WIDTH + p * LANES:GLA_QK_WIDTH + (p + 1) * LANES]
        v = v_ref[bb, rows, p * 2 * GLA_DV:(p + 1) * 2 * GLA_DV]
        g_hi = g.astype(BF16)
        g_lo = (g - g_hi.astype(F32)).astype(BF16)
        e2 = jnp.dot(emat_ref[d], jnp.concatenate([g_hi, g_lo], axis=1), preferred_element_type=F32)
        e = e2[:, 0:LANES] + e2[:, LANES:2 * LANES]
        b = e[0:C]
        end_row = C - 1 if d == 0 else 0
        b_end = b[end_row:end_row + 1]
        st = st_ref[bb, d, p]
        o = lax.dot_general((q * jnp.exp(b)).astype(BF16), st.astype(BF16), NT_DIMS,
                            preferred_element_type=F32)
        att = lax.dot_general(q.astype(BF16), head_block(k.astype(BF16)), NT_DIMS,
                              preferred_element_type=F32) * lmask_ref[d, GLA_LEVELS]
        for lvl in range(GLA_LEVELS):
            m = C >> (lvl + 1)
            if lvl < GLA_ROW_LEVELS:
                pieces = []
                for u in range(C // (2 * m)):
                    row = 2 * m * u + (m - 1 if d == 0 else m)
                    pieces.append(jnp.broadcast_to(b[row:row + 1], (2 * m, LANES)))
                rho = pieces[0] if len(pieces) == 1 else jnp.concatenate(pieces, axis=0)
            else:
                rho = e[(lvl - GLA_ROW_LEVELS + 1) * C:(lvl - GLA_ROW_LEVELS + 2) * C]
            x = jnp.exp(-jnp.abs(b - rho))
            att = att + lax.dot_general((q * x).astype(BF16), head_block((k * x).astype(BF16)), NT_DIMS,
                                        preferred_element_type=F32) * lmask_ref[d, lvl]
        vzero = jnp.zeros_like(v)
        v_bd = jnp.concatenate([jnp.where(v_lo, v, vzero), jnp.where(v_lo, vzero, v)], axis=0)
        o = o + jnp.dot(att.astype(BF16), v_bd, preferred_element_type=F32)
        o_ref[bb, rows, p * 2 * GLA_DV:(p + 1) * 2 * GLA_DV] = o.astype(BF16)
        ks = (k * jnp.exp(b_end - b)).astype(BF16)
        upd = lax.dot_general(v, ks, TN_DIMS, preferred_element_type=F32)
        st_ref[bb, d, p] = st * jnp.exp(b_end) + jnp.where(st_keep, upd, 0.0)

    n_chunks = TILE // C
    for step in range(n_chunks):
        for bb in range(BP):
            for p in range(GLA_HEADS // 2):
                chunk(bb, 0, p, step * C, qkf_ref, laf_ref, vf_ref, of_ref)
                chunk(bb, 1, p, (n_chunks - 1 - step) * C, qkb_ref, lab_ref, vb_ref, ob_ref)


def _gla_call(gqk, la, gv, emat, lmask):
    B = gqk.shape[0]
    fwd = lambda b, i: (b, i, 0)
    bwd = lambda b, i: (b, jnp.where(i == 0, 0, N_TILES - i), 0)
    bwd_la = lambda b, i: (b, jnp.where(i == 0, 0, N_TILES - i), 1)
    return pl.pallas_call(
        _gla_kernel,
        grid=(B // BP, N_TILES),
        in_specs=[
            pl.BlockSpec((2, E_ROWS, GLA_CHUNK), lambda b, i: (0, 0, 0)),
            pl.BlockSpec((2, GLA_LEVELS + 1, GLA_CHUNK, LANES), lambda b, i: (0, 0, 0, 0)),
            pl.BlockSpec((BP, TILE, 512), fwd),
            pl.BlockSpec((BP, TILE, 256), fwd),
            pl.BlockSpec((BP, TILE, 512), fwd),
            pl.BlockSpec((BP, TILE, 512), bwd),
            pl.BlockSpec((BP, TILE, 256), bwd_la),
            pl.BlockSpec((BP, TILE, 512), bwd),
        ],
        out_specs=[pl.BlockSpec((BP, TILE, 512), fwd), pl.BlockSpec((BP, TILE, 512), bwd)],
        out_shape=[jax.ShapeDtypeStruct((B, TOK, 512), BF16)] * 2,
        scratch_shapes=[pltpu.VMEM((BP, 2, GLA_HEADS // 2, 2 * GLA_DV, LANES), F32)],
        compiler_params=_cparams(("arbitrary", "arbitrary")),
        name="gla_scan",
    )(emat, lmask, gqk, la, gv, gqk, la, gv)


def _outproj_kernel(xc_ref, xl_ref, mod_ref, na_ref, wa_ref, of_ref, ob_ref, go_ref, gn_ref, w_ref, o_ref,
                    *, two_source):
    x = _pick_tile(xc_ref, xl_ref) if two_source else xl_ref[...]
    og = _rows2d(of_ref[...].astype(F32) + ob_ref[...].astype(F32))
    gn = gn_ref[...]
    parts = []
    for h in range(GLA_HEADS):
        seg = og[:, h * GLA_DV:(h + 1) * GLA_DV]
        ms = jnp.mean(seg * seg, axis=-1, keepdims=True)
        parts.append(seg * lax.rsqrt(ms + EPS) * gn)
    go = _rows2d(go_ref[...].astype(F32))
    y_g = (jnp.concatenate(parts, axis=1) * (go * jax.nn.sigmoid(go))).astype(BF16)
    res = (jnp.dot(_rows2d(na_ref[...]), w_ref[0:256, :], preferred_element_type=F32)
           + jnp.dot(_rows2d(wa_ref[...]), w_ref[256:512, :], preferred_element_type=F32)
           + jnp.dot(y_g, w_ref[512:1024, :], preferred_element_type=F32))
    gate = mod_ref[...][:, :, 2 * D_MODEL:3 * D_MODEL]
    o_ref[...] = x + gate * res.reshape(BP, TILE, D_MODEL)


def _outproj_call(x_first, x_rest, rest_tile0, mod_rows, o_na, o_wa, o_f, o_b, go, gn, w_out, with_ctx):
    B = x_first.shape[0]
    t0 = 0 if with_ctx else 1
    n_t = N_TILES - t0
    full = lambda w: pl.BlockSpec((BP, TILE, w), lambda b, i: (b, i + t0, 0))
    own = lambda w: pl.BlockSpec((BP, TILE, w), lambda b, i: (b, i, 0))
    if with_ctx:
        spec_c, spec_l = _two_source_specs(rest_tile0)
    else:
        spec_c = spec_l = full(D_MODEL)
    return pl.pallas_call(
        functools.partial(_outproj_kernel, two_source=with_ctx),
        grid=(B // BP, n_t),
        in_specs=[
            spec_c, spec_l,
            _mod_spec(t0),
            own(NA_WIDTH), own(WA_Q_WIDTH), full(512), full(512), full(512),
            pl.BlockSpec((1, GLA_DV), lambda b, i: (0, 0)),
            pl.BlockSpec((MIX_WIDTH, D_MODEL), lambda b, i: (0, 0)),
        ],
        out_specs=own(D_MODEL),
        out_shape=jax.ShapeDtypeStruct((B, n_t * TILE, D_MODEL), F32),
        compiler_params=_cparams(("arbitrary", "arbitrary")),
        name="out_proj",
    )(x_first, x_rest, mod_rows, o_na, o_wa, o_f, o_b, go, gn, w_out)


HALO = SUBLANES


def _ffn_kernel(x_ref, xp_ref, xn_ref, mod_ref, g2_ref, wv_ref, wg_ref, cw_ref, cb_ref, wd_ref,
                fg_ref, o_ref, gs_ref, *, t0, final):
    ti = pl.program_id(1) + t0
    mod = mod_ref[...]
    shift, scale = mod[:, :, 3 * D_MODEL:4 * D_MODEL], mod[:, :, 4 * D_MODEL:5 * D_MODEL]
    g2 = g2_ref[...]
    x = x_ref[...]
    h_t = _mod_norm(x, g2, shift, scale)
    h_ext = jnp.concatenate([_mod_norm(xp_ref[...], g2, shift, scale), h_t,
                             _mod_norm(xn_ref[...], g2, shift, scale)], axis=1)
    h_ext = _rows2d(h_ext).astype(BF16)
    h_b = _rows2d(h_t).astype(BF16)
    ext = TILE + 2 * HALO
    seq_start = ti <= 1
    seq_end = (ti == 0) | (ti == N_TILES - 1)
    acc = jnp.zeros((BP * TILE, D_MODEL), F32)
    for c in range(FF_DIM // FF_CHUNK):
        cols = slice(c * FF_CHUNK, (c + 1) * FF_CHUNK)
        g_ext = jnp.dot(h_ext, wg_ref[:, cols], preferred_element_type=F32).reshape(BP, ext, FF_CHUNK)
        gs_ref[...] = g_ext
        gs_ref[:, 0:HALO, :] = jnp.where(seq_start, 0.0, g_ext[:, 0:HALO])
        gs_ref[:, HALO + TILE:, :] = jnp.where(seq_end, 0.0, g_ext[:, HALO + TILE:])
        cw = cw_ref[:, cols]
        gate = (cw[0:1] * gs_ref[:, pl.ds(HALO - 1, TILE), :] + cw[1:2] * gs_ref[:, pl.ds(HALO, TILE), :]
                + cw[2:3] * gs_ref[:, pl.ds(HALO + 1, TILE), :] + cb_ref[:, cols])
        gate = _rows2d(gate)
        val = jnp.dot(h_b, wv_ref[:, cols], preferred_element_type=F32)
        act = 0.5 * gate * (1.0 + lax.erf(gate * (1.0 / math.sqrt(2.0))))
        acc = acc + jnp.dot((act * val).astype(BF16), wd_ref[cols, :], preferred_element_type=F32)
    y = x + mod[:, :, 5 * D_MODEL:6 * D_MODEL] * acc.reshape(BP, TILE, D_MODEL)
    if final:
        ms = jnp.mean(y * y, axis=-1, keepdims=True)
        y = y * lax.rsqrt(ms + EPS) * fg_ref[...]
    o_ref[...] = y


def _ffn_call(x1, mod_rows, g2, w_up, cw, cb, wd, fg, with_ctx, final):
    B = x1.shape[0]
    t0 = 0 if with_ctx else 1
    n_t = x1.shape[1] // TILE
    per8 = TILE // HALO
    last8 = x1.shape[1] // HALO - 1
    const = lambda r, c: pl.BlockSpec((r, c), lambda b, i: (0, 0))
    once = pl.Buffered(1)
    return pl.pallas_call(
        functools.partial(_ffn_kernel, t0=t0, final=final),
        grid=(B // BP, n_t),
        in_specs=[
            pl.BlockSpec((BP, TILE, D_MODEL), lambda b, i: (b, i, 0)),
            pl.BlockSpec((BP, HALO, D_MODEL), lambda b, i: (b, jnp.maximum(i * per8 - 1, 0), 0)),
            pl.BlockSpec((BP, HALO, D_MODEL), lambda b, i: (b, jnp.minimum((i + 1) * per8, last8), 0)),
            _mod_spec(t0),
            const(1, D_MODEL),
            pl.BlockSpec((D_MODEL, FF_DIM), lambda b, i: (0, 0), pipeline_mode=once),
            pl.BlockSpec((D_MODEL, FF_DIM), lambda b, i: (0, 1), pipeline_mode=once),
            const(3, FF_DIM), const(1, FF_DIM),
            pl.BlockSpec((FF_DIM, D_MODEL), lambda b, i: (0, 0), pipeline_mode=once),
            const(1, D_MODEL),
        ],
        out_specs=pl.BlockSpec((BP, TILE, D_MODEL), lambda b, i: (b, i, 0)),
        out_shape=jax.ShapeDtypeStruct(x1.shape, F32),
        scratch_shapes=[pltpu.VMEM((BP, TILE + 2 * HALO, FF_CHUNK), F32)],
        compiler_params=_cparams(("arbitrary", "arbitrary")),
        name="conv_ffn",
    )(x1, x1, x1, mod_rows, g2, w_up, w_up, cw, cb, wd, fg)


def _rope_tables():
    t = np.arange(SEQ)
    n_freq = WA_HEAD_DIM // 4
    inv = ROPE_THETA ** (-np.arange(n_freq, dtype=np.float32) / n_freq)
    ang = np.concatenate([(t // GRID_W).astype(np.float32)[:, None] * inv,
                          (t % GRID_W).astype(np.float32)[:, None] * inv], axis=-1)
    cos = np.tile(np.cos(ang), (1, 2 * WA_HEADS))
    sin = np.tile(np.sin(ang), (1, 2 * WA_HEADS))
    cos = np.concatenate([np.ones((CTX_LEN, 256), np.float32), cos], axis=0)
    sin = np.concatenate([np.zeros((CTX_LEN, 256), np.float32), sin], axis=0)
    return jnp.asarray(cos, F32), jnp.asarray(sin, F32)


def _rotate_half_cols(w, n_heads):
    w = w.reshape(w.shape[0], n_heads, 2, WA_HEAD_DIM // 2)
    return jnp.stack([-w[:, :, 1], w[:, :, 0]], axis=2).reshape(w.shape[0], n_heads * WA_HEAD_DIM)


def _layer_weights(w_in, gate_w, gate_b, w_out):
    waq = w_in[:, 3 * NA_WIDTH:3 * NA_WIDTH + WA_Q_WIDTH].reshape(D_MODEL, WA_HEADS, WA_HEAD_DIM)
    waq = waq[:, np.array(WA_HEAD_ORDER)].reshape(D_MODEL, WA_Q_WIDTH)
    wak = w_in[:, W_WAK:W_WAV]
    lr_pad = jnp.zeros((D_MODEL, X_COLS - X_LR - 2 * GLA_GATE_RANK), w_in.dtype)
    w_extra = jnp.concatenate([waq, _rotate_half_cols(waq, WA_HEADS), _rotate_half_cols(wak, WA_KV_HEADS),
                               w_in[:, W_LR:], lr_pad], axis=1).astype(BF16)
    zero = jnp.zeros((GLA_GATE_RANK, GLA_QK_WIDTH), F32)
    gw = jnp.concatenate([
        jnp.concatenate([gate_w[0], zero], axis=1),
        jnp.concatenate([zero, gate_w[1]], axis=1),
        jnp.zeros((LANES - 2 * GLA_GATE_RANK, 2 * GLA_QK_WIDTH), F32)], axis=0).astype(BF16)
    gb = gate_b.reshape(1, 2 * GLA_QK_WIDTH)
    wo_wa = w_out[NA_WIDTH:NA_WIDTH + WA_Q_WIDTH].reshape(WA_HEADS, WA_HEAD_DIM, D_MODEL)[np.array(WA_HEAD_ORDER)]
    wo = jnp.concatenate([w_out[:NA_WIDTH], wo_wa.reshape(WA_Q_WIDTH, D_MODEL),
                          w_out[NA_WIDTH + WA_Q_WIDTH:]], axis=0).astype(BF16)
    return w_in.astype(BF16), w_extra, gw, gb, wo


def kernel(x, c, ctx, c_ctx, w_mod, b_mod, norm1_g, norm2_g, w_in, na_rpb, wa_sink, gla_gate_w,
           gla_gate_b, gla_norm_g, w_out, ffn_w_up, ffn_conv_w, ffn_conv_b, ffn_w_down, final_norm_g):
    B = x.shape[0]
    assert x.shape == (B, SEQ, D_MODEL) and ctx.shape == (B, CTX_LEN, D_MODEL) and B % BP == 0
    n_c = -(-(B + 1) // SUBLANES) * SUBLANES
    cc = jnp.concatenate([c, c_ctx[None], jnp.zeros((n_c - B - 1, D_MODEL), F32)], axis=0)
    mods = _mod_call(cc, w_mod, b_mod)
    cos_t, sin_t = _rope_tables()
    emat, lmask = _gla_constants()
    emat = jnp.asarray(emat, BF16)
    lmask = jnp.asarray(lmask, F32)
    fg = final_norm_g.reshape(1, D_MODEL)

    x_first, x_rest, rest_tile0 = ctx, x, 0
    out = None
    for l in range(DEPTH):
        last = l == DEPTH - 1
        mod_rows = jnp.stack([jnp.broadcast_to(mods[l, B], (B, 6 * D_MODEL)), mods[l, :B]],
                             axis=0).reshape(2, B, 1, 6 * D_MODEL)
        w_main, w_extra, gw, gb, wo = _layer_weights(w_in[l], gla_gate_w[l], gla_gate_b[l], w_out[l])
        na, wa, gqk, gv, go, la = _inproj_call(x_first, x_rest, rest_tile0, mod_rows,
                                               norm1_g[l].reshape(1, D_MODEL), w_main, w_extra, gw, gb, cos_t, sin_t)
        bias = _nabias_call(na_rpb[l])
        o_na = _na_call(na, bias, with_ctx=not last)
        o_wa = _wa_call(wa, wa_sink[l], with_ctx=not last)
        o_f, o_b = _gla_call(gqk, la, gv, emat, lmask)
        x1 = _outproj_call(x_first, x_rest, rest_tile0, mod_rows, o_na, o_wa, o_f, o_b, go,
                           gla_norm_g[l].reshape(1, GLA_DV), wo, with_ctx=not last)
        x2 = _ffn_call(x1, mod_rows, norm2_g[l].reshape(1, D_MODEL), ffn_w_up[l].astype(BF16),
                       ffn_conv_w[l], ffn_conv_b[l].reshape(1, FF_DIM), ffn_w_down[l].astype(BF16), fg,
                       with_ctx=not last, final=last)
        if last:
            out = x2
        else:
            x_first, x_rest, rest_tile0 = x2, x2, 1
    return out
```

```python
import functools
import math

import numpy as np
import jax
import jax.numpy as jnp
from jax import lax
from jax.experimental import pallas as pl
from jax.experimental.pallas import tpu as pltpu

D_MODEL = 1024
SEQ = 2048
DEPTH = 2
CTX_LEN = 256
GRID_W = 64
GRID_ROWS = SEQ // GRID_W
TOK = CTX_LEN + SEQ

NA_HEADS = 4
NA_HEAD_DIM = 64
NA_WIN_R = 8
NA_WIN_C = 16
WA_HEADS = 4
WA_KV_HEADS = 2
WA_HEAD_DIM = 64
WA_WINDOW = 128
WA_BLOCK = 128
GLA_HEADS = 4
GLA_DK = 64
GLA_DV = 128
GLA_GATE_RANK = 16
GLA_GATE_TAU = 16.0
GLA_CHUNK = 64

NA_WIDTH = NA_HEADS * NA_HEAD_DIM
WA_Q_WIDTH = WA_HEADS * WA_HEAD_DIM
WA_KV_WIDTH = WA_KV_HEADS * WA_HEAD_DIM
GLA_QK_WIDTH = GLA_HEADS * GLA_DK
GLA_V_WIDTH = GLA_HEADS * GLA_DV
MIX_WIDTH = NA_WIDTH + WA_Q_WIDTH + GLA_V_WIDTH
IN_WIDTH = 3 * NA_WIDTH + WA_Q_WIDTH + 2 * WA_KV_WIDTH + 2 * GLA_QK_WIDTH + 2 * GLA_V_WIDTH + 2 * GLA_GATE_RANK

FF_DIM = 2816
FF_CHUNK = 1408
ROPE_THETA = 10000.0
EPS = 1e-6

BP = 2
TILE = 256
N_TILES = TOK // TILE
LANES = 128
SUBLANES = 8
NEG = -1e30
VMEM_LIMIT = 56 * 1024 * 1024

F32 = jnp.float32
BF16 = jnp.bfloat16
NT_DIMS = (((1,), (1,)), ((), ()))
TN_DIMS = (((0,), (0,)), ((), ()))

W_NA = 0
W_WAK = 1024
W_WAV = 1152
W_GQ = 1280
W_GK = 1536
W_GV = 1792
W_GO = 2304
W_LR = 2816
X_WAQ = 0
X_WAQS = 256
X_WAKS = 512
X_LR = 640
X_COLS = 768

WA_HEAD_ORDER = (0, 2, 1, 3)
Q_SCALE = NA_HEAD_DIM ** -0.5


def _cparams(sem):
    return pltpu.CompilerParams(dimension_semantics=sem, vmem_limit_bytes=VMEM_LIMIT)


def _lane_lo(shape, width=LANES):
    lane = lax.broadcasted_iota(jnp.int32, shape, len(shape) - 1)
    return (lane & (width - 1)) < (width // 2)


def _mod_kernel(c_ref, w_ref, b_ref, o_ref):
    c = c_ref[...]
    sc = c * jax.nn.sigmoid(c)
    o_ref[...] = jnp.dot(sc.astype(BF16), w_ref[...].astype(BF16),
                         preferred_element_type=F32) + b_ref[...]


def _mod_call(cc, w_mod, b_mod):
    tn = 1536
    rows = cc.shape[0]
    return pl.pallas_call(
        _mod_kernel,
        grid=(DEPTH, 6 * D_MODEL // tn),
        in_specs=[
            pl.BlockSpec((rows, D_MODEL), lambda l, j: (0, 0)),
            pl.BlockSpec((None, D_MODEL, tn), lambda l, j: (l, 0, j)),
            pl.BlockSpec((None, 1, tn), lambda l, j: (l, 0, j)),
        ],
        out_specs=pl.BlockSpec((None, rows, tn), lambda l, j: (l, 0, j)),
        out_shape=jax.ShapeDtypeStruct((DEPTH, rows, 6 * D_MODEL), F32),
        compiler_params=_cparams(("arbitrary", "arbitrary")),
        name="adaln_mod",
    )(cc, w_mod, b_mod.reshape(DEPTH, 1, 6 * D_MODEL))


N_RO = 2 * NA_WIN_R - 1
N_CO = 2 * NA_WIN_C - 1
N_BIAS = N_RO + 1


def _nabias_kernel(rpb_ref, o_ref):
    h = pl.program_id(0)
    shape = (GRID_W, LANES)
    q = lax.broadcasted_iota(jnp.int32, shape, 0)
    kc = lax.broadcasted_iota(jnp.int32, shape, 1) & (GRID_W - 1)
    d = kc - q + (NA_WIN_C - 1)
    cs = jnp.clip(q - NA_WIN_C // 2, 0, GRID_W - NA_WIN_C)
    inwin = (kc >= cs) & (kc < cs + NA_WIN_C)
    base = h * (N_RO * N_CO)
    o_ref[0] = jnp.full(shape, NEG, F32)
    for ro in range(N_RO):
        acc = jnp.full(shape, NEG, F32)
        for dd in range(N_CO):
            acc = jnp.where(d == dd, rpb_ref[base + ro * N_CO + dd], acc)
        o_ref[1 + ro] = jnp.where(inwin, acc, NEG)


def _nabias_call(rpb):
    return pl.pallas_call(
        _nabias_kernel,
        grid=(NA_HEADS,),
        in_specs=[pl.BlockSpec(memory_space=pltpu.SMEM)],
        out_specs=pl.BlockSpec((None, N_BIAS, GRID_W, LANES), lambda h: (h, 0, 0, 0)),
        out_shape=jax.ShapeDtypeStruct((NA_HEADS, N_BIAS, GRID_W, LANES), F32),
        compiler_params=_cparams(("arbitrary",)),
        name="na_bias",
    )(rpb.reshape(-1))


def _mod_norm(x, g, shift, scale):
    ms = jnp.mean(x * x, axis=-1, keepdims=True)
    return (x * lax.rsqrt(ms + EPS) * g) * (1.0 + scale) + shift


def _pick_tile(first_ref, rest_ref):
    return jnp.where(pl.program_id(1) == 0, first_ref[...], rest_ref[...])


def _rows2d(a):
    return a.reshape(a.shape[0] * a.shape[1], a.shape[2])


def _inproj_kernel(xc_ref, xl_ref, mod_ref, g1_ref, w_ref, wx_ref, gw_ref, gb_ref, cos_ref, sin_ref,
                   na_ref, wa_ref, gqk_ref, gv_ref, go_ref, la_ref):
    mod = mod_ref[...]
    h = _mod_norm(_pick_tile(xc_ref, xl_ref), g1_ref[...], mod[:, :, 0:D_MODEL], mod[:, :, D_MODEL:2 * D_MODEL])
    hb = _rows2d(h).astype(BF16)

    def proj(w, c0, c1):
        return jnp.dot(hb, w[:, c0:c1], preferred_element_type=F32).reshape(BP, TILE, c1 - c0)

    na_ref[:, :, 0:NA_WIDTH] = (proj(w_ref, W_NA, W_NA + NA_WIDTH) * Q_SCALE).astype(BF16)
    na_ref[:, :, NA_WIDTH:3 * NA_WIDTH] = proj(w_ref, W_NA + NA_WIDTH, W_NA + 3 * NA_WIDTH).astype(BF16)
    cos = cos_ref[...]
    sin = sin_ref[...]
    wa_ref[:, :, 0:256] = ((proj(wx_ref, X_WAQ, X_WAQS) * cos + proj(wx_ref, X_WAQS, X_WAKS) * sin)
                           * Q_SCALE).astype(BF16)
    wa_ref[:, :, 256:384] = (proj(w_ref, W_WAK, W_WAV) * cos[:, 0:LANES]
                             + proj(wx_ref, X_WAKS, X_LR) * sin[:, 0:LANES]).astype(BF16)
    wa_ref[:, :, 384:512] = proj(w_ref, W_WAV, W_GQ).astype(BF16)
    gqk_ref[:, :, 0:GLA_QK_WIDTH] = proj(w_ref, W_GQ, W_GK) * Q_SCALE
    gqk_ref[:, :, GLA_QK_WIDTH:] = proj(w_ref, W_GK, W_GV)
    gv_ref[...] = proj(w_ref, W_GV, W_GO).astype(BF16)
    go_ref[...] = proj(w_ref, W_GO, W_LR).astype(BF16)
    lr = _rows2d(proj(wx_ref, X_LR, X_COLS)).astype(BF16)
    logit = jnp.dot(lr, gw_ref[...], preferred_element_type=F32) + gb_ref[...]
    log_sig = jnp.minimum(logit, 0.0) - jnp.log(1.0 + jnp.exp(-jnp.abs(logit)))
    la_ref[...] = (log_sig * (1.0 / GLA_GATE_TAU)).reshape(BP, TILE, 2 * GLA_QK_WIDTH)


def _two_source_specs(rest_tile0):
    return (pl.BlockSpec((BP, TILE, D_MODEL), lambda b, i: (b, 0, 0)),
            pl.BlockSpec((BP, TILE, D_MODEL), lambda b, i: (b, jnp.maximum(i - 1, 0) + rest_tile0, 0)))


def _mod_spec(t0):
    return pl.BlockSpec((None, BP, 1, 6 * D_MODEL), lambda b, i: (jnp.minimum(i + t0, 1), b, 0, 0))


def _inproj_call(x_first, x_rest, rest_tile0, mod_rows, g1, w_main, w_extra, gw, gb, cos_t, sin_t):
    B = x_first.shape[0]
    tile = lambda w: pl.BlockSpec((BP, TILE, w), lambda b, i: (b, i, 0))
    const = lambda r, c: pl.BlockSpec((r, c), lambda b, i: (0, 0))
    spec_c, spec_l = _two_source_specs(rest_tile0)
    return pl.pallas_call(
        _inproj_kernel,
        grid=(B // BP, N_TILES),
        in_specs=[
            spec_c, spec_l,
            _mod_spec(0),
            const(1, D_MODEL),
            const(D_MODEL, IN_WIDTH),
            const(D_MODEL, X_COLS),
            const(LANES, 2 * GLA_QK_WIDTH),
            const(1, 2 * GLA_QK_WIDTH),
            pl.BlockSpec((TILE, 256), lambda b, i: (i, 0)),
            pl.BlockSpec((TILE, 256), lambda b, i: (i, 0)),
        ],
        out_specs=[tile(768), tile(512), tile(512), tile(512), tile(512), tile(512)],
        out_shape=[
            jax.ShapeDtypeStruct((B, TOK, 768), BF16),
            jax.ShapeDtypeStruct((B, TOK, 512), BF16),
            jax.ShapeDtypeStruct((B, TOK, 512), F32),
            jax.ShapeDtypeStruct((B, TOK, 512), BF16),
            jax.ShapeDtypeStruct((B, TOK, 512), BF16),
            jax.ShapeDtypeStruct((B, TOK, 512), F32),
        ],
        compiler_params=_cparams(("arbitrary", "arbitrary")),
        name="in_proj",
    )(x_first, x_rest, mod_rows, g1, w_main, w_extra, gw, gb, cos_t, sin_t)


NA_ROWS_PER_STEP = TILE // GRID_W
NA_STEPS = GRID_ROWS // NA_ROWS_PER_STEP
NA_UNION = NA_WIN_R + NA_ROWS_PER_STEP


def _split_heads_rows(q):
    lo = _lane_lo(q.shape)
    zero = jnp.zeros_like(q)
    return jnp.concatenate([jnp.where(lo, q, zero), jnp.where(lo, zero, q)], axis=0)


def _merge_heads_rows(o, n):
    return jnp.where(_lane_lo((n, LANES)), o[0:n], o[n:2 * n])


def _na_rows(j, slab_ref, bias_ref, o_ref):
    n_union = NA_UNION * GRID_W
    u0 = jnp.clip(j * NA_ROWS_PER_STEP - NA_WIN_R // 2, 0, GRID_ROWS - NA_UNION)
    qrow = pl.multiple_of(CTX_LEN + j * TILE, TILE)
    krow = pl.multiple_of(CTX_LEN + u0 * GRID_W, GRID_W)
    lo = _lane_lo((GRID_W, LANES))
    tile_idx = []
    for a in range(NA_ROWS_PER_STEP):
        r = j * NA_ROWS_PER_STEP + a
        rs = jnp.clip(r - NA_WIN_R // 2, 0, GRID_ROWS - NA_WIN_R)
        idx = []
        for u in range(NA_UNION):
            key_row = u0 + u
            inside = (key_row >= rs) & (key_row < rs + NA_WIN_R)
            idx.append(jnp.where(inside, key_row - r + NA_WIN_R, 0))
        tile_idx.append(idx)
    for p in range(NA_HEADS // 2):
        kc = slab_ref[0:CTX_LEN, 256 + p * LANES:256 + (p + 1) * LANES]
        vc = slab_ref[0:CTX_LEN, 512 + p * LANES:512 + (p + 1) * LANES]
        ku = slab_ref[pl.ds(krow, n_union), 256 + p * LANES:256 + (p + 1) * LANES]
        vu = slab_ref[pl.ds(krow, n_union), 512 + p * LANES:512 + (p + 1) * LANES]
        q_all = slab_ref[pl.ds(qrow, TILE), p * LANES:(p + 1) * LANES]
        q2 = jnp.concatenate([_split_heads_rows(q_all[a * GRID_W:(a + 1) * GRID_W])
                              for a in range(NA_ROWS_PER_STEP)], axis=0)
        bias = jnp.concatenate([
            jnp.concatenate([jnp.where(lo, bias_ref[2 * p + hh, tile_idx[a][2 * jj]],
                                       bias_ref[2 * p + hh, tile_idx[a][2 * jj + 1]])
                             for jj in range(NA_UNION // 2)], axis=1)
            for a in range(NA_ROWS_PER_STEP) for hh in range(2)], axis=0)
        s_u = lax.dot_general(q2, ku, NT_DIMS, preferred_element_type=F32) + bias
        s_c = lax.dot_general(q2, kc, NT_DIMS, preferred_element_type=F32)
        m = jnp.maximum(jnp.max(s_u, axis=-1, keepdims=True), jnp.max(s_c, axis=-1, keepdims=True))
        p_u = jnp.exp(s_u - m)
        p_c = jnp.exp(s_c - m)
        den = jnp.sum(p_u, axis=-1, keepdims=True) + jnp.sum(p_c, axis=-1, keepdims=True)
        o = (jnp.dot(p_u.astype(BF16), vu, preferred_element_type=F32)
             + jnp.dot(p_c.astype(BF16), vc, preferred_element_type=F32)) / den
        out = [_merge_heads_rows(o[a * 2 * GRID_W:(a + 1) * 2 * GRID_W], GRID_W)
               for a in range(NA_ROWS_PER_STEP)]
        o_ref[:, p * LANES:(p + 1) * LANES] = jnp.concatenate(out, axis=0).astype(BF16)


def _na_ctx(slab_ref, o_ref):
    for p in range(NA_HEADS // 2):
        kc = slab_ref[0:CTX_LEN, 256 + p * LANES:256 + (p + 1) * LANES]
        vc = slab_ref[0:CTX_LEN, 512 + p * LANES:512 + (p + 1) * LANES]
        q2 = _split_heads_rows(slab_ref[0:CTX_LEN, p * LANES:(p + 1) * LANES])
        s_c = lax.dot_general(q2, kc, NT_DIMS, preferred_element_type=F32)
        p_c = jnp.exp(s_c - jnp.max(s_c, axis=-1, keepdims=True))
        den = jnp.sum(p_c, axis=-1, keepdims=True)
        o = jnp.dot(p_c.astype(BF16), vc, preferred_element_type=F32) / den
        o_ref[:, p * LANES:(p + 1) * LANES] = _merge_heads_rows(o, CTX_LEN).astype(BF16)


N_WA_BLOCKS = SEQ // WA_BLOCK
WA_PER_STEP = TILE // WA_BLOCK
WA_KCOL = slice(256, 384)
WA_VCOL = slice(384, 512)


def _wa_stack_heads(slab_ref, qrow):
    lo = _lane_lo((WA_BLOCK, LANES))
    qa = slab_ref[pl.ds(qrow, WA_BLOCK), 0:LANES]
    qb = slab_ref[pl.ds(qrow, WA_BLOCK), LANES:2 * LANES]
    zero = jnp.zeros_like(qa)
    return jnp.concatenate([jnp.where(lo, qa, zero), jnp.where(lo, zero, qa),
                            jnp.where(lo, qb, zero), jnp.where(lo, zero, qb)], axis=0)


def _wa_softmax_pv(sink_ref, scores, values, o_ref, out_row):
    rows4 = WA_HEADS * WA_BLOCK
    slot = lax.broadcasted_iota(jnp.int32, (rows4, 1), 0) // WA_BLOCK
    sink = jnp.zeros((rows4, 1), F32)
    for s_i, h in enumerate(WA_HEAD_ORDER):
        sink = jnp.where(slot == s_i, sink_ref[h], sink)
    m = sink
    for s in scores:
        m = jnp.maximum(m, jnp.max(s, axis=-1, keepdims=True))
    den = jnp.exp(sink - m)
    o = None
    for s, v in zip(scores, values):
        e = jnp.exp(s - m)
        den = den + jnp.sum(e, axis=-1, keepdims=True)
        t = jnp.dot(e.astype(BF16), v, preferred_element_type=F32)
        o = t if o is None else o + t
    o = o / den
    b = WA_BLOCK
    lo = _lane_lo((b, LANES))
    rows = slice(out_row, out_row + b)
    o_ref[rows, 0:LANES] = jnp.where(lo, o[0:b], o[b:2 * b]).astype(BF16)
    o_ref[rows, LANES:2 * LANES] = jnp.where(lo, o[2 * b:3 * b], o[3 * b:4 * b]).astype(BF16)


def _wa_scores(q4, k):
    return lax.dot_general(q4, k, NT_DIMS, preferred_element_type=F32)


def _wa_block(n, out_row, sink_ref, slab_ref, o_ref):
    rows4 = WA_HEADS * WA_BLOCK
    qrow = pl.multiple_of(CTX_LEN + n * WA_BLOCK, WA_BLOCK)
    prow = pl.multiple_of(qrow - WA_BLOCK, WA_BLOCK)
    nrow = pl.multiple_of(jnp.minimum(qrow + WA_BLOCK, TOK - WA_BLOCK), WA_BLOCK)
    k_blocks = [slab_ref[pl.ds(s, WA_BLOCK), WA_KCOL] for s in (prow, qrow, nrow)]
    v_blocks = [slab_ref[pl.ds(s, WA_BLOCK), WA_VCOL] for s in (prow, qrow, nrow)]
    i = lax.broadcasted_iota(jnp.int32, (rows4, WA_BLOCK), 0) & (WA_BLOCK - 1)
    jj = lax.broadcasted_iota(jnp.int32, (rows4, WA_BLOCK), 1)
    keep_prev = (jj >= i) & (n > 0)
    keep_next = (jj <= i) & (n < N_WA_BLOCKS - 1)
    q4 = _wa_stack_heads(slab_ref, qrow)
    s_p = jnp.where(keep_prev, _wa_scores(q4, k_blocks[0]), NEG)
    s_m = _wa_scores(q4, k_blocks[1])
    s_n = jnp.where(keep_next, _wa_scores(q4, k_blocks[2]), NEG)
    s_c = _wa_scores(q4, slab_ref[0:CTX_LEN, WA_KCOL])
    _wa_softmax_pv(sink_ref, [s_p, s_m, s_n, s_c], v_blocks + [slab_ref[0:CTX_LEN, WA_VCOL]], o_ref, out_row)


def _wa_ctx(blk, sink_ref, slab_ref, o_ref):
    q4 = _wa_stack_heads(slab_ref, blk * WA_BLOCK)
    s_c = _wa_scores(q4, slab_ref[0:CTX_LEN, WA_KCOL])
    _wa_softmax_pv(sink_ref, [s_c], [slab_ref[0:CTX_LEN, WA_VCOL]], o_ref, blk * WA_BLOCK)


def _attn_kernel(sink_ref, na_ref, wa_ref, bias_ref, ona_ref, owa_ref):
    j = pl.program_id(1)

    @pl.when(j < NA_STEPS)
    def _():
        _na_rows(j, na_ref, bias_ref, ona_ref)
        for qb in range(WA_PER_STEP):
            _wa_block(j * WA_PER_STEP + qb, qb * WA_BLOCK, sink_ref, wa_ref, owa_ref)

    @pl.when(j >= NA_STEPS)
    def _():
        _na_ctx(na_ref, ona_ref)
        for blk in range(CTX_LEN // WA_BLOCK):
            _wa_ctx(blk, sink_ref, wa_ref, owa_ref)


def _attn_call(na, wa, bias, sink, with_ctx):
    B = na.shape[0]
    n_ctx = 1 if with_ctx else 0

    def out_map(b, j):
        if with_ctx:
            return (b, jnp.where(j < NA_STEPS, j + 1, 0), 0)
        return (b, j, 0)

    out_rows = SEQ + n_ctx * CTX_LEN
    return pl.pallas_call(
        _attn_kernel,
        grid=(B, NA_STEPS + n_ctx),
        in_specs=[
            pl.BlockSpec(memory_space=pltpu.SMEM),
            pl.BlockSpec((None, TOK, 768), lambda b, j: (b, 0, 0)),
            pl.BlockSpec((None, TOK, 512), lambda b, j: (b, 0, 0)),
            pl.BlockSpec((NA_HEADS, N_BIAS, GRID_W, LANES), lambda b, j: (0, 0, 0, 0)),
        ],
        out_specs=[pl.BlockSpec((None, TILE, NA_WIDTH), out_map), pl.BlockSpec((None, TILE, WA_Q_WIDTH), out_map)],
        out_shape=[jax.ShapeDtypeStruct((B, out_rows, NA_WIDTH), BF16),
                   jax.ShapeDtypeStruct((B, out_rows, WA_Q_WIDTH), BF16)],
        compiler_params=_cparams(("arbitrary", "arbitrary")),
        name="softmax_attn",
    )(sink, na, wa, bias)


GLA_LEVELS = int(math.log2(GLA_CHUNK))
GLA_ROW_LEVELS = 3
E_ROWS = (1 + GLA_LEVELS - GLA_ROW_LEVELS) * GLA_CHUNK


def _gla_constants():
    C = GLA_CHUNK
    p = np.arange(C)
    r = p[None, :]
    mats = [r <= p[:, None]]
    masks = []
    for lvl in range(GLA_LEVELS):
        m = C >> (lvl + 1)
        pair = p // (2 * m)
        half = (p // m) % 2
        a_end = pair * 2 * m + m - 1
        if lvl >= GLA_ROW_LEVELS:
            mats.append(r <= a_end[:, None])
        masks.append((pair[:, None] == pair[None, :]) & (half[:, None] == 1) & (half[None, :] == 0))
    masks.append(p[:, None] == p[None, :])
    fwd = np.concatenate(mats, axis=0).astype(np.float32)
    bwd = np.concatenate([blk[::-1, ::-1] for blk in mats], axis=0).astype(np.float32)
    msk = np.stack(masks).astype(np.float32)
    lmask = np.stack([msk, msk[:, ::-1, ::-1]])
    lmask = np.concatenate([lmask, lmask], axis=-1)
    return np.stack([fwd, bwd]), lmask


def _gla_kernel(emat_ref, lmask_ref, qkf_ref, laf_ref, vf_ref, qkb_ref, lab_ref, vb_ref,
                of_ref, ob_ref, st_ref):
    i = pl.program_id(1)

    @pl.when(i == 0)
    def _():
        st_ref[...] = jnp.zeros_like(st_ref)

    C = GLA_CHUNK
    lo = _lane_lo((C, LANES))
    v_lo = _lane_lo((C, 2 * GLA_DV), width=2 * GLA_DV)
    st_rows = lax.broadcasted_iota(jnp.int32, (2 * GLA_DV, LANES), 0) < GLA_DV
    st_keep = st_rows == _lane_lo((2 * GLA_DV, LANES))

    def head_block(kk):
        zero = jnp.zeros_like(kk)
        return jnp.concatenate([jnp.where(lo, kk, zero), jnp.where(lo, zero, kk)], axis=0)

    def chunk(bb, d, p, r0, qk_ref, la_ref, v_ref, o_ref):
        rows = slice(r0, r0 + C)
        g = la_ref[bb, rows, p * LANES:(p + 1) * LANES]
        q = qk_ref[bb, rows, p * LANES:(p + 1) * LANES]
        k = qk_ref[bb, rows, GLA_QK_WIDTH + p * LANES:GLA_QK_WIDTH + (p + 1) * LANES]
        v = v_ref[bb, rows, p * 2 * GLA_DV:(p + 1) * 2 * GLA_DV]
        g_hi = g.astype(BF16)
        g_lo = (g - g_hi.astype(F32)).astype(BF16)
        e2 = jnp.dot(emat_ref[d], jnp.concatenate([g_hi, g_lo], axis=1), preferred_element_type=F32)
        e = e2[:, 0:LANES] + e2[:, LANES:2 * LANES]
        b = e[0:C]
        end_row = C - 1 if d == 0 else 0
        b_end = b[end_row:end_row + 1]
        st = st_ref[bb, d, p]
        o = lax.dot_general((q * jnp.exp(b)).astype(BF16), st.astype(BF16), NT_DIMS,
                            preferred_element_type=F32)
        att = lax.dot_general(q.astype(BF16), head_block(k.astype(BF16)), NT_DIMS,
                              preferred_element_type=F32) * lmask_ref[d, GLA_LEVELS]
        for lvl in range(GLA_LEVELS):
            m = C >> (lvl + 1)
            if lvl < GLA_ROW_LEVELS:
                pieces = []
                for u in range(C // (2 * m)):
                    row = 2 * m * u + (m - 1 if d == 0 else m)
                    pieces.append(jnp.broadcast_to(b[row:row + 1], (2 * m, LANES)))
                rho = pieces[0] if len(pieces) == 1 else jnp.concatenate(pieces, axis=0)
            else:
                rho = e[(lvl - GLA_ROW_LEVELS + 1) * C:(lvl - GLA_ROW_LEVELS + 2) * C]
            x = jnp.exp(-jnp.abs(b - rho))
            att = att + lax.dot_general((q * x).astype(BF16), head_block((k * x).astype(BF16)), NT_DIMS,
                                        preferred_element_type=F32) * lmask_ref[d, lvl]
        vzero = jnp.zeros_like(v)
        v_bd = jnp.concatenate([jnp.where(v_lo, v, vzero), jnp.where(v_lo, vzero, v)], axis=0)
        o = o + jnp.dot(att.astype(BF16), v_bd, preferred_element_type=F32)
        o_ref[bb, rows, p * 2 * GLA_DV:(p + 1) * 2 * GLA_DV] = o.astype(BF16)
        ks = (k * jnp.exp(b_end - b)).astype(BF16)
        upd = lax.dot_general(v, ks, TN_DIMS, preferred_element_type=F32)
        st_ref[bb, d, p] = st * jnp.exp(b_end) + jnp.where(st_keep, upd, 0.0)

    n_chunks = TILE // C
    for step in range(n_chunks):
        for bb in range(BP):
            for p in range(GLA_HEADS // 2):
                chunk(bb, 0, p, step * C, qkf_ref, laf_ref, vf_ref, of_ref)
                chunk(bb, 1, p, (n_chunks - 1 - step) * C, qkb_ref, lab_ref, vb_ref, ob_ref)


def _gla_call(gqk, la, gv, emat, lmask):
    B = gqk.shape[0]
    fwd = lambda b, i: (b, i, 0)
    bwd = lambda b, i: (b, jnp.where(i == 0, 0, N_TILES - i), 0)
    bwd_la = lambda b, i: (b, jnp.where(i == 0, 0, N_TILES - i), 1)
    return pl.pallas_call(
        _gla_kernel,
        grid=(B // BP, N_TILES),
        in_specs=[
            pl.BlockSpec((2, E_ROWS, GLA_CHUNK), lambda b, i: (0, 0, 0)),
            pl.BlockSpec((2, GLA_LEVELS + 1, GLA_CHUNK, LANES), lambda b, i: (0, 0, 0, 0)),
            pl.BlockSpec((BP, TILE, 512), fwd),
            pl.BlockSpec((BP, TILE, 256), fwd),
            pl.BlockSpec((BP, TILE, 512), fwd),
            pl.BlockSpec((BP, TILE, 512), bwd),
            pl.BlockSpec((BP, TILE, 256), bwd_la),
            pl.BlockSpec((BP, TILE, 512), bwd),
        ],
        out_specs=[pl.BlockSpec((BP, TILE, 512), fwd), pl.BlockSpec((BP, TILE, 512), bwd)],
        out_shape=[jax.ShapeDtypeStruct((B, TOK, 512), BF16)] * 2,
        scratch_shapes=[pltpu.VMEM((BP, 2, GLA_HEADS // 2, 2 * GLA_DV, LANES), F32)],
        compiler_params=_cparams(("arbitrary", "arbitrary")),
        name="gla_scan",
    )(emat, lmask, gqk, la, gv, gqk, la, gv)


def _outproj_kernel(xc_ref, xl_ref, mod_ref, na_ref, wa_ref, of_ref, ob_ref, go_ref, gn_ref, w_ref, o_ref,
                    *, two_source):
    x = _pick_tile(xc_ref, xl_ref) if two_source else xl_ref[...]
    og = _rows2d(of_ref[...].astype(F32) + ob_ref[...].astype(F32))
    gn = gn_ref[...]
    parts = []
    for h in range(GLA_HEADS):
        seg = og[:, h * GLA_DV:(h + 1) * GLA_DV]
        ms = jnp.mean(seg * seg, axis=-1, keepdims=True)
        parts.append(seg * lax.rsqrt(ms + EPS) * gn)
    go = _rows2d(go_ref[...].astype(F32))
    y_g = (jnp.concatenate(parts, axis=1) * (go * jax.nn.sigmoid(go))).astype(BF16)
    res = (jnp.dot(_rows2d(na_ref[...]), w_ref[0:256, :], preferred_element_type=F32)
           + jnp.dot(_rows2d(wa_ref[...]), w_ref[256:512, :], preferred_element_type=F32)
           + jnp.dot(y_g, w_ref[512:1024, :], preferred_element_type=F32))
    gate = mod_ref[...][:, :, 2 * D_MODEL:3 * D_MODEL]
    o_ref[...] = x + gate * res.reshape(BP, TILE, D_MODEL)


def _outproj_call(x_first, x_rest, rest_tile0, mod_rows, o_na, o_wa, o_f, o_b, go, gn, w_out, with_ctx):
    B = x_first.shape[0]
    t0 = 0 if with_ctx else 1
    n_t = N_TILES - t0
    full = lambda w: pl.BlockSpec((BP, TILE, w), lambda b, i: (b, i + t0, 0))
    own = lambda w: pl.BlockSpec((BP, TILE, w), lambda b, i: (b, i, 0))
    if with_ctx:
        spec_c, spec_l = _two_source_specs(rest_tile0)
    else:
        spec_c = spec_l = full(D_MODEL)
    return pl.pallas_call(
        functools.partial(_outproj_kernel, two_source=with_ctx),
        grid=(B // BP, n_t),
        in_specs=[
            spec_c, spec_l,
            _mod_spec(t0),
            own(NA_WIDTH), own(WA_Q_WIDTH), full(512), full(512), full(512),
            pl.BlockSpec((1, GLA_DV), lambda b, i: (0, 0)),
            pl.BlockSpec((MIX_WIDTH, D_MODEL), lambda b, i: (0, 0)),
        ],
        out_specs=own(D_MODEL),
        out_shape=jax.ShapeDtypeStruct((B, n_t * TILE, D_MODEL), F32),
        compiler_params=_cparams(("arbitrary", "arbitrary")),
        name="out_proj",
    )(x_first, x_rest, mod_rows, o_na, o_wa, o_f, o_b, go, gn, w_out)


HALO = SUBLANES


def _ffn_kernel(x_ref, xp_ref, xn_ref, mod_ref, g2_ref, wv_ref, wg_ref, cw_ref, cb_ref, wd_ref,
                fg_ref, o_ref, gs_ref, *, t0, final):
    ti = pl.program_id(1) + t0
    mod = mod_ref[...]
    shift, scale = mod[:, :, 3 * D_MODEL:4 * D_MODEL], mod[:, :, 4 * D_MODEL:5 * D_MODEL]
    g2 = g2_ref[...]
    x = x_ref[...]
    h_t = _mod_norm(x, g2, shift, scale)
    h_ext = jnp.concatenate([_mod_norm(xp_ref[...], g2, shift, scale), h_t,
                             _mod_norm(xn_ref[...], g2, shift, scale)], axis=1)
    h_ext = _rows2d(h_ext).astype(BF16)
    h_b = _rows2d(h_t).astype(BF16)
    ext = TILE + 2 * HALO
    seq_start = ti <= 1
    seq_end = (ti == 0) | (ti == N_TILES - 1)
    acc = jnp.zeros((BP * TILE, D_MODEL), F32)
    for c in range(FF_DIM // FF_CHUNK):
        cols = slice(c * FF_CHUNK, (c + 1) * FF_CHUNK)
        g_ext = jnp.dot(h_ext, wg_ref[:, cols], preferred_element_type=F32).reshape(BP, ext, FF_CHUNK)
        gs_ref[...] = g_ext
        gs_ref[:, 0:HALO, :] = jnp.where(seq_start, 0.0, g_ext[:, 0:HALO])
        gs_ref[:, HALO + TILE:, :] = jnp.where(seq_end, 0.0, g_ext[:, HALO + TILE:])
        cw = cw_ref[:, cols]
        gate = (cw[0:1] * gs_ref[:, pl.ds(HALO - 1, TILE), :] + cw[1:2] * gs_ref[:, pl.ds(HALO, TILE), :]
                + cw[2:3] * gs_ref[:, pl.ds(HALO + 1, TILE), :] + cb_ref[:, cols])
        gate = _rows2d(gate)
        val = jnp.dot(h_b, wv_ref[:, cols], preferred_element_type=F32)
        act = 0.5 * gate * (1.0 + lax.erf(gate * (1.0 / math.sqrt(2.0))))
        acc = acc + jnp.dot((act * val).astype(BF16), wd_ref[cols, :], preferred_element_type=F32)
    y = x + mod[:, :, 5 * D_MODEL:6 * D_MODEL] * acc.reshape(BP, TILE, D_MODEL)
    if final:
        ms = jnp.mean(y * y, axis=-1, keepdims=True)
        y = y * lax.rsqrt(ms + EPS) * fg_ref[...]
    o_ref[...] = y


def _ffn_call(x1, mod_rows, g2, w_up, cw, cb, wd, fg, with_ctx, final):
    B = x1.shape[0]
    t0 = 0 if with_ctx else 1
    n_t = x1.shape[1] // TILE
    per8 = TILE // HALO
    last8 = x1.shape[1] // HALO - 1
    const = lambda r, c: pl.BlockSpec((r, c), lambda b, i: (0, 0))
    once = pl.Buffered(1)
    return pl.pallas_call(
        functools.partial(_ffn_kernel, t0=t0, final=final),
        grid=(B // BP, n_t),
        in_specs=[
            pl.BlockSpec((BP, TILE, D_MODEL), lambda b, i: (b, i, 0)),
            pl.BlockSpec((BP, HALO, D_MODEL), lambda b, i: (b, jnp.maximum(i * per8 - 1, 0), 0)),
            pl.BlockSpec((BP, HALO, D_MODEL), lambda b, i: (b, jnp.minimum((i + 1) * per8, last8), 0)),
            _mod_spec(t0),
            const(1, D_MODEL),
            pl.BlockSpec((D_MODEL, FF_DIM), lambda b, i: (0, 0), pipeline_mode=once),
            pl.BlockSpec((D_MODEL, FF_DIM), lambda b, i: (0, 1), pipeline_mode=once),
            const(3, FF_DIM), const(1, FF_DIM),
            pl.BlockSpec((FF_DIM, D_MODEL), lambda b, i: (0, 0), pipeline_mode=once),
            const(1, D_MODEL),
        ],
        out_specs=pl.BlockSpec((BP, TILE, D_MODEL), lambda b, i: (b, i, 0)),
        out_shape=jax.ShapeDtypeStruct(x1.shape, F32),
        scratch_shapes=[pltpu.VMEM((BP, TILE + 2 * HALO, FF_CHUNK), F32)],
        compiler_params=_cparams(("arbitrary", "arbitrary")),
        name="conv_ffn",
    )(x1, x1, x1, mod_rows, g2, w_up, w_up, cw, cb, wd, fg)


def _rope_tables():
    t = np.arange(SEQ)
    n_freq = WA_HEAD_DIM // 4
    inv = ROPE_THETA ** (-np.arange(n_freq, dtype=np.float32) / n_freq)
    ang = np.concatenate([(t // GRID_W).astype(np.float32)[:, None] * inv,
                          (t % GRID_W).astype(np.float32)[:, None] * inv], axis=-1)
    cos = np.tile(np.cos(ang), (1, 2 * WA_HEADS))
    sin = np.tile(np.sin(ang), (1, 2 * WA_HEADS))
    cos = np.concatenate([np.ones((CTX_LEN, 256), np.float32), cos], axis=0)
    sin = np.concatenate([np.zeros((CTX_LEN, 256), np.float32), sin], axis=0)
    return jnp.asarray(cos, F32), jnp.asarray(sin, F32)


def _rotate_half_cols(w, n_heads):
    w = w.reshape(w.shape[0], n_heads, 2, WA_HEAD_DIM // 2)
    return jnp.stack([-w[:, :, 1], w[:, :, 0]], axis=2).reshape(w.shape[0], n_heads * WA_HEAD_DIM)


def _layer_weights(w_in, gate_w, gate_b, w_out):
    waq = w_in[:, 3 * NA_WIDTH:3 * NA_WIDTH + WA_Q_WIDTH].reshape(D_MODEL, WA_HEADS, WA_HEAD_DIM)
    waq = waq[:, np.array(WA_HEAD_ORDER)].reshape(D_MODEL, WA_Q_WIDTH)
    wak = w_in[:, W_WAK:W_WAV]
    lr_pad = jnp.zeros((D_MODEL, X_COLS - X_LR - 2 * GLA_GATE_RANK), w_in.dtype)
    w_extra = jnp.concatenate([waq, _rotate_half_cols(waq, WA_HEADS), _rotate_half_cols(wak, WA_KV_HEADS),
                               w_in[:, W_LR:], lr_pad], axis=1).astype(BF16)
    zero = jnp.zeros((GLA_GATE_RANK, GLA_QK_WIDTH), F32)
    gw = jnp.concatenate([
        jnp.concatenate([gate_w[0], zero], axis=1),
        jnp.concatenate([zero, gate_w[1]], axis=1),
        jnp.zeros((LANES - 2 * GLA_GATE_RANK, 2 * GLA_QK_WIDTH), F32)], axis=0).astype(BF16)
    gb = gate_b.reshape(1, 2 * GLA_QK_WIDTH)
    wo_wa = w_out[NA_WIDTH:NA_WIDTH + WA_Q_WIDTH].reshape(WA_HEADS, WA_HEAD_DIM, D_MODEL)[np.array(WA_HEAD_ORDER)]
    wo = jnp.concatenate([w_out[:NA_WIDTH], wo_wa.reshape(WA_Q_WIDTH, D_MODEL),
                          w_out[NA_WIDTH + WA_Q_WIDTH:]], axis=0).astype(BF16)
    return w_in.astype(BF16), w_extra, gw, gb, wo


def kernel(x, c, ctx, c_ctx, w_mod, b_mod, norm1_g, norm2_g, w_in, na_rpb, wa_sink, gla_gate_w,
           gla_gate_b, gla_norm_g, w_out, ffn_w_up, ffn_conv_w, ffn_conv_b, ffn_w_down, final_norm_g):
    B = x.shape[0]
    assert x.shape == (B, SEQ, D_MODEL) and ctx.shape == (B, CTX_LEN, D_MODEL) and B % BP == 0
    n_c = -(-(B + 1) // SUBLANES) * SUBLANES
    cc = jnp.concatenate([c, c_ctx[None], jnp.zeros((n_c - B - 1, D_MODEL), F32)], axis=0)
    mods = _mod_call(cc, w_mod, b_mod)
    cos_t, sin_t = _rope_tables()
    emat, lmask = _gla_constants()
    emat = jnp.asarray(emat, BF16)
    lmask = jnp.asarray(lmask, F32)
    fg = final_norm_g.reshape(1, D_MODEL)

    x_first, x_rest, rest_tile0 = ctx, x, 0
    out = None
    for l in range(DEPTH):
        last = l == DEPTH - 1
        mod_rows = jnp.stack([jnp.broadcast_to(mods[l, B], (B, 6 * D_MODEL)), mods[l, :B]],
                             axis=0).reshape(2, B, 1, 6 * D_MODEL)
        w_main, w_extra, gw, gb, wo = _layer_weights(w_in[l], gla_gate_w[l], gla_gate_b[l], w_out[l])
        na, wa, gqk, gv, go, la = _inproj_call(x_first, x_rest, rest_tile0, mod_rows,
                                               norm1_g[l].reshape(1, D_MODEL), w_main, w_extra, gw, gb, cos_t, sin_t)
        bias = _nabias_call(na_rpb[l])
        o_na, o_wa = _attn_call(na, wa, bias, wa_sink[l], with_ctx=not last)
        o_f, o_b = _gla_call(gqk, la, gv, emat, lmask)
        x1 = _outproj_call(x_first, x_rest, rest_tile0, mod_rows, o_na, o_wa, o_f, o_b, go,
                           gla_norm_g[l].reshape(1, GLA_DV), wo, with_ctx=not last)
        x2 = _ffn_call(x1, mod_rows, norm2_g[l].reshape(1, D_MODEL), ffn_w_up[l].astype(BF16),
                       ffn_conv_w[l], ffn_conv_b[l].reshape(1, FF_DIM), ffn_w_down[l].astype(BF16), fg,
                       with_ctx=not last, final=last)
        if last:
            out = x2
        else:
            x_first, x_rest, rest_tile0 = x2, x2, 1
    return out
```

```python
import functools
import math

import numpy as np
import jax
import jax.numpy as jnp
from jax import lax
from jax.experimental import pallas as pl
from jax.experimental.pallas import tpu as pltpu

D_MODEL = 1024
SEQ = 2048
DEPTH = 2
CTX_LEN = 256
GRID_W = 64
GRID_ROWS = SEQ // GRID_W
TOK = CTX_LEN + SEQ

NA_HEADS = 4
NA_HEAD_DIM = 64
NA_WIN_R = 8
NA_WIN_C = 16
WA_HEADS = 4
WA_KV_HEADS = 2
WA_HEAD_DIM = 64
WA_WINDOW = 128
WA_BLOCK = 128
GLA_HEADS = 4
GLA_DK = 64
GLA_DV = 128
GLA_GATE_RANK = 16
GLA_GATE_TAU = 16.0
GLA_CHUNK = 64

NA_WIDTH = NA_HEADS * NA_HEAD_DIM
WA_Q_WIDTH = WA_HEADS * WA_HEAD_DIM
WA_KV_WIDTH = WA_KV_HEADS * WA_HEAD_DIM
GLA_QK_WIDTH = GLA_HEADS * GLA_DK
GLA_V_WIDTH = GLA_HEADS * GLA_DV
MIX_WIDTH = NA_WIDTH + WA_Q_WIDTH + GLA_V_WIDTH
IN_WIDTH = 3 * NA_WIDTH + WA_Q_WIDTH + 2 * WA_KV_WIDTH + 2 * GLA_QK_WIDTH + 2 * GLA_V_WIDTH + 2 * GLA_GATE_RANK

FF_DIM = 2816
FF_CHUNK = 1408
ROPE_THETA = 10000.0
EPS = 1e-6

BP = 2
TILE = 256
N_TILES = TOK // TILE
LANES = 128
SUBLANES = 8
NEG = -1e30
VMEM_LIMIT = 56 * 1024 * 1024

F32 = jnp.float32
BF16 = jnp.bfloat16
NT_DIMS = (((1,), (1,)), ((), ()))
TN_DIMS = (((0,), (0,)), ((), ()))

W_NA = 0
W_WAK = 1024
W_WAV = 1152
W_GQ = 1280
W_GK = 1536
W_GV = 1792
W_GO = 2304
W_LR = 2816
X_WAQ = 0
X_WAQS = 256
X_WAKS = 512
X_LR = 640
X_COLS = 768

WA_HEAD_ORDER = (0, 2, 1, 3)
Q_SCALE = NA_HEAD_DIM ** -0.5


def _cparams(sem):
    return pltpu.CompilerParams(dimension_semantics=sem, vmem_limit_bytes=VMEM_LIMIT)


def _lane_lo(shape, width=LANES):
    lane = lax.broadcasted_iota(jnp.int32, shape, len(shape) - 1)
    return (lane & (width - 1)) < (width // 2)


def _mod_kernel(c_ref, w_ref, b_ref, o_ref):
    c = c_ref[...]
    sc = c * jax.nn.sigmoid(c)
    o_ref[...] = jnp.dot(sc.astype(BF16), w_ref[...].astype(BF16),
                         preferred_element_type=F32) + b_ref[...]


def _mod_call(cc, w_mod, b_mod):
    tn = 1536
    rows = cc.shape[0]
    return pl.pallas_call(
        _mod_kernel,
        grid=(DEPTH, 6 * D_MODEL // tn),
        in_specs=[
            pl.BlockSpec((rows, D_MODEL), lambda l, j: (0, 0)),
            pl.BlockSpec((None, D_MODEL, tn), lambda l, j: (l, 0, j)),
            pl.BlockSpec((None, 1, tn), lambda l, j: (l, 0, j)),
        ],
        out_specs=pl.BlockSpec((None, rows, tn), lambda l, j: (l, 0, j)),
        out_shape=jax.ShapeDtypeStruct((DEPTH, rows, 6 * D_MODEL), F32),
        compiler_params=_cparams(("arbitrary", "arbitrary")),
        name="adaln_mod",
    )(cc, w_mod, b_mod.reshape(DEPTH, 1, 6 * D_MODEL))


N_RO = 2 * NA_WIN_R - 1
N_CO = 2 * NA_WIN_C - 1
N_BIAS = N_RO + 1


def _nabias_kernel(rpb_ref, o_ref):
    h = pl.program_id(0)
    shape = (GRID_W, LANES)
    q = lax.broadcasted_iota(jnp.int32, shape, 0)
    kc = lax.broadcasted_iota(jnp.int32, shape, 1) & (GRID_W - 1)
    d = kc - q + (NA_WIN_C - 1)
    cs = jnp.clip(q - NA_WIN_C // 2, 0, GRID_W - NA_WIN_C)
    inwin = (kc >= cs) & (kc < cs + NA_WIN_C)
    base = h * (N_RO * N_CO)
    o_ref[0] = jnp.full(shape, NEG, F32)
    for ro in range(N_RO):
        acc = jnp.full(shape, NEG, F32)
        for dd in range(N_CO):
            acc = jnp.where(d == dd, rpb_ref[base + ro * N_CO + dd], acc)
        o_ref[1 + ro] = jnp.where(inwin, acc, NEG)


def _nabias_call(rpb):
    return pl.pallas_call(
        _nabias_kernel,
        grid=(NA_HEADS,),
        in_specs=[pl.BlockSpec(memory_space=pltpu.SMEM)],
        out_specs=pl.BlockSpec((None, N_BIAS, GRID_W, LANES), lambda h: (h, 0, 0, 0)),
        out_shape=jax.ShapeDtypeStruct((NA_HEADS, N_BIAS, GRID_W, LANES), F32),
        compiler_params=_cparams(("arbitrary",)),
        name="na_bias",
    )(rpb.reshape(-1))


def _mod_norm(x, g, shift, scale):
    ms = jnp.mean(x * x, axis=-1, keepdims=True)
    return (x * lax.rsqrt(ms + EPS) * g) * (1.0 + scale) + shift


def _pick_tile(first_ref, rest_ref):
    return jnp.where(pl.program_id(1) == 0, first_ref[...], rest_ref[...])


def _rows2d(a):
    return a.reshape(a.shape[0] * a.shape[1], a.shape[2])


def _inproj_kernel(xc_ref, xl_ref, mod_ref, g1_ref, w_ref, wx_ref, gw_ref, gb_ref, cos_ref, sin_ref,
                   na_ref, wa_ref, gqk_ref, gv_ref, go_ref, la_ref):
    mod = mod_ref[...]
    h = _mod_norm(_pick_tile(xc_ref, xl_ref), g1_ref[...], mod[:, :, 0:D_MODEL], mod[:, :, D_MODEL:2 * D_MODEL])
    hb = _rows2d(h).astype(BF16)

    def proj(w, c0, c1):
        return jnp.dot(hb, w[:, c0:c1], preferred_element_type=F32).reshape(BP, TILE, c1 - c0)

    na_ref[:, :, 0:NA_WIDTH] = (proj(w_ref, W_NA, W_NA + NA_WIDTH) * Q_SCALE).astype(BF16)
    na_ref[:, :, NA_WIDTH:3 * NA_WIDTH] = proj(w_ref, W_NA + NA_WIDTH, W_NA + 3 * NA_WIDTH).astype(BF16)
    cos = cos_ref[...]
    sin = sin_ref[...]
    wa_ref[:, :, 0:256] = ((proj(wx_ref, X_WAQ, X_WAQS) * cos + proj(wx_ref, X_WAQS, X_WAKS) * sin)
                           * Q_SCALE).astype(BF16)
    wa_ref[:, :, 256:384] = (proj(w_ref, W_WAK, W_WAV) * cos[:, 0:LANES]
                             + proj(wx_ref, X_WAKS, X_LR) * sin[:, 0:LANES]).astype(BF16)
    wa_ref[:, :, 384:512] = proj(w_ref, W_WAV, W_GQ).astype(BF16)
    gqk_ref[:, :, 0:GLA_QK_WIDTH] = proj(w_ref, W_GQ, W_GK) * Q_SCALE
    gqk_ref[:, :, GLA_QK_WIDTH:] = proj(w_ref, W_GK, W_GV)
    gv_ref[...] = proj(w_ref, W_GV, W_GO).astype(BF16)
    go_ref[...] = proj(w_ref, W_GO, W_LR).astype(BF16)
    lr = _rows2d(proj(wx_ref, X_LR, X_COLS)).astype(BF16)
    logit = jnp.dot(lr, gw_ref[...], preferred_element_type=F32) + gb_ref[...]
    log_sig = jnp.minimum(logit, 0.0) - jnp.log(1.0 + jnp.exp(-jnp.abs(logit)))
    la_ref[...] = (log_sig * (1.0 / GLA_GATE_TAU)).reshape(BP, TILE, 2 * GLA_QK_WIDTH)


def _two_source_specs(rest_tile0):
    return (pl.BlockSpec((BP, TILE, D_MODEL), lambda b, i: (b, 0, 0)),
            pl.BlockSpec((BP, TILE, D_MODEL), lambda b, i: (b, jnp.maximum(i - 1, 0) + rest_tile0, 0)))


def _mod_spec(t0):
    return pl.BlockSpec((None, BP, 1, 6 * D_MODEL), lambda b, i: (jnp.minimum(i + t0, 1), b, 0, 0))


def _inproj_call(x_first, x_rest, rest_tile0, mod_rows, g1, w_main, w_extra, gw, gb, cos_t, sin_t):
    B = x_first.shape[0]
    tile = lambda w: pl.BlockSpec((BP, TILE, w), lambda b, i: (b, i, 0))
    const = lambda r, c: pl.BlockSpec((r, c), lambda b, i: (0, 0))
    spec_c, spec_l = _two_source_specs(rest_tile0)
    return pl.pallas_call(
        _inproj_kernel,
        grid=(B // BP, N_TILES),
        in_specs=[
            spec_c, spec_l,
            _mod_spec(0),
            const(1, D_MODEL),
            const(D_MODEL, IN_WIDTH),
            const(D_MODEL, X_COLS),
            const(LANES, 2 * GLA_QK_WIDTH),
            const(1, 2 * GLA_QK_WIDTH),
            pl.BlockSpec((TILE, 256), lambda b, i: (i, 0)),
            pl.BlockSpec((TILE, 256), lambda b, i: (i, 0)),
        ],
        out_specs=[tile(768), tile(512), tile(512), tile(512), tile(512), tile(512)],
        out_shape=[
            jax.ShapeDtypeStruct((B, TOK, 768), BF16),
            jax.ShapeDtypeStruct((B, TOK, 512), BF16),
            jax.ShapeDtypeStruct((B, TOK, 512), F32),
            jax.ShapeDtypeStruct((B, TOK, 512), BF16),
            jax.ShapeDtypeStruct((B, TOK, 512), BF16),
            jax.ShapeDtypeStruct((B, TOK, 512), F32),
        ],
        compiler_params=_cparams(("arbitrary", "arbitrary")),
        name="in_proj",
    )(x_first, x_rest, mod_rows, g1, w_main, w_extra, gw, gb, cos_t, sin_t)


NA_ROWS_PER_STEP = TILE // GRID_W
NA_STEPS = GRID_ROWS // NA_ROWS_PER_STEP
NA_UNION = NA_WIN_R + NA_ROWS_PER_STEP


def _split_heads_rows(q):
    lo = _lane_lo(q.shape)
    zero = jnp.zeros_like(q)
    return jnp.concatenate([jnp.where(lo, q, zero), jnp.where(lo, zero, q)], axis=0)


def _merge_heads_rows(o, n):
    return jnp.where(_lane_lo((n, LANES)), o[0:n], o[n:2 * n])


def _na_rows(j, slab_ref, bias_ref, o_ref):
    n_union = NA_UNION * GRID_W
    u0 = jnp.clip(j * NA_ROWS_PER_STEP - NA_WIN_R // 2, 0, GRID_ROWS - NA_UNION)
    qrow = pl.multiple_of(CTX_LEN + j * TILE, TILE)
    krow = pl.multiple_of(CTX_LEN + u0 * GRID_W, GRID_W)
    lo = _lane_lo((GRID_W, LANES))
    tile_idx = []
    for a in range(NA_ROWS_PER_STEP):
        r = j * NA_ROWS_PER_STEP + a
        rs = jnp.clip(r - NA_WIN_R // 2, 0, GRID_ROWS - NA_WIN_R)
        idx = []
        for u in range(NA_UNION):
            key_row = u0 + u
            inside = (key_row >= rs) & (key_row < rs + NA_WIN_R)
            idx.append(jnp.where(inside, key_row - r + NA_WIN_R, 0))
        tile_idx.append(idx)
    for p in range(NA_HEADS // 2):
        kc = slab_ref[0:CTX_LEN, 256 + p * LANES:256 + (p + 1) * LANES]
        vc = slab_ref[0:CTX_LEN, 512 + p * LANES:512 + (p + 1) * LANES]
        ku = slab_ref[pl.ds(krow, n_union), 256 + p * LANES:256 + (p + 1) * LANES]
        vu = slab_ref[pl.ds(krow, n_union), 512 + p * LANES:512 + (p + 1) * LANES]
        q_all = slab_ref[pl.ds(qrow, TILE), p * LANES:(p + 1) * LANES]
        q2 = jnp.concatenate([_split_heads_rows(q_all[a * GRID_W:(a + 1) * GRID_W])
                              for a in range(NA_ROWS_PER_STEP)], axis=0)
        bias = jnp.concatenate([
            jnp.concatenate([jnp.where(lo, bias_ref[2 * p + hh, tile_idx[a][2 * jj]],
                                       bias_ref[2 * p + hh, tile_idx[a][2 * jj + 1]])
                             for jj in range(NA_UNION // 2)], axis=1)
            for a in range(NA_ROWS_PER_STEP) for hh in range(2)], axis=0)
        s_u = lax.dot_general(q2, ku, NT_DIMS, preferred_element_type=F32) + bias
        s_c = lax.dot_general(q2, kc, NT_DIMS, preferred_element_type=F32)
        m = jnp.maximum(jnp.max(s_u, axis=-1, keepdims=True), jnp.max(s_c, axis=-1, keepdims=True))
        p_u = jnp.exp(s_u - m)
        p_c = jnp.exp(s_c - m)
        den = jnp.sum(p_u, axis=-1, keepdims=True) + jnp.sum(p_c, axis=-1, keepdims=True)
        o = (jnp.dot(p_u.astype(BF16), vu, preferred_element_type=F32)
             + jnp.dot(p_c.astype(BF16), vc, preferred_element_type=F32)) / den
        out = [_merge_heads_rows(o[a * 2 * GRID_W:(a + 1) * 2 * GRID_W], GRID_W)
               for a in range(NA_ROWS_PER_STEP)]
        o_ref[:, p * LANES:(p + 1) * LANES] = jnp.concatenate(out, axis=0).astype(BF16)


def _na_ctx(slab_ref, o_ref):
    for p in range(NA_HEADS // 2):
        kc = slab_ref[0:CTX_LEN, 256 + p * LANES:256 + (p + 1) * LANES]
        vc = slab_ref[0:CTX_LEN, 512 + p * LANES:512 + (p + 1) * LANES]
        q2 = _split_heads_rows(slab_ref[0:CTX_LEN, p * LANES:(p + 1) * LANES])
        s_c = lax.dot_general(q2, kc, NT_DIMS, preferred_element_type=F32)
        p_c = jnp.exp(s_c - jnp.max(s_c, axis=-1, keepdims=True))
        den = jnp.sum(p_c, axis=-1, keepdims=True)
        o = jnp.dot(p_c.astype(BF16), vc, preferred_element_type=F32) / den
        o_ref[:, p * LANES:(p + 1) * LANES] = _merge_heads_rows(o, CTX_LEN).astype(BF16)


N_WA_BLOCKS = SEQ // WA_BLOCK
WA_PER_STEP = TILE // WA_BLOCK
WA_KCOL = slice(256, 384)
WA_VCOL = slice(384, 512)


def _wa_stack_heads(slab_ref, qrow):
    lo = _lane_lo((WA_BLOCK, LANES))
    qa = slab_ref[pl.ds(qrow, WA_BLOCK), 0:LANES]
    qb = slab_ref[pl.ds(qrow, WA_BLOCK), LANES:2 * LANES]
    zero = jnp.zeros_like(qa)
    return jnp.concatenate([jnp.where(lo, qa, zero), jnp.where(lo, zero, qa),
                            jnp.where(lo, qb, zero), jnp.where(lo, zero, qb)], axis=0)


def _wa_softmax_pv(sink_ref, scores, values, o_ref, out_row):
    rows4 = WA_HEADS * WA_BLOCK
    slot = lax.broadcasted_iota(jnp.int32, (rows4, 1), 0) // WA_BLOCK
    sink = jnp.zeros((rows4, 1), F32)
    for s_i, h in enumerate(WA_HEAD_ORDER):
        sink = jnp.where(slot == s_i, sink_ref[h], sink)
    m = sink
    for s in scores:
        m = jnp.maximum(m, jnp.max(s, axis=-1, keepdims=True))
    den = jnp.exp(sink - m)
    o = None
    for s, v in zip(scores, values):
        e = jnp.exp(s - m)
        den = den + jnp.sum(e, axis=-1, keepdims=True)
        t = jnp.dot(e.astype(BF16), v, preferred_element_type=F32)
        o = t if o is None else o + t
    o = o / den
    b = WA_BLOCK
    lo = _lane_lo((b, LANES))
    rows = slice(out_row, out_row + b)
    o_ref[rows, 0:LANES] = jnp.where(lo, o[0:b], o[b:2 * b]).astype(BF16)
    o_ref[rows, LANES:2 * LANES] = jnp.where(lo, o[2 * b:3 * b], o[3 * b:4 * b]).astype(BF16)


def _wa_scores(q4, k):
    return lax.dot_general(q4, k, NT_DIMS, preferred_element_type=F32)


def _wa_block(n, out_row, sink_ref, slab_ref, o_ref):
    rows4 = WA_HEADS * WA_BLOCK
    qrow = pl.multiple_of(CTX_LEN + n * WA_BLOCK, WA_BLOCK)
    prow = pl.multiple_of(qrow - WA_BLOCK, WA_BLOCK)
    nrow = pl.multiple_of(jnp.minimum(qrow + WA_BLOCK, TOK - WA_BLOCK), WA_BLOCK)
    k_blocks = [slab_ref[pl.ds(s, WA_BLOCK), WA_KCOL] for s in (prow, qrow, nrow)]
    v_blocks = [slab_ref[pl.ds(s, WA_BLOCK), WA_VCOL] for s in (prow, qrow, nrow)]
    i = lax.broadcasted_iota(jnp.int32, (rows4, WA_BLOCK), 0) & (WA_BLOCK - 1)
    jj = lax.broadcasted_iota(jnp.int32, (rows4, WA_BLOCK), 1)
    keep_prev = (jj >= i) & (n > 0)
    keep_next = (jj <= i) & (n < N_WA_BLOCKS - 1)
    q4 = _wa_stack_heads(slab_ref, qrow)
    s_p = jnp.where(keep_prev, _wa_scores(q4, k_blocks[0]), NEG)
    s_m = _wa_scores(q4, k_blocks[1])
    s_n = jnp.where(keep_next, _wa_scores(q4, k_blocks[2]), NEG)
    s_c = _wa_scores(q4, slab_ref[0:CTX_LEN, WA_KCOL])
    _wa_softmax_pv(sink_ref, [s_p, s_m, s_n, s_c], v_blocks + [slab_ref[0:CTX_LEN, WA_VCOL]], o_ref, out_row)


def _wa_ctx(blk, sink_ref, slab_ref, o_ref):
    q4 = _wa_stack_heads(slab_ref, blk * WA_BLOCK)
    s_c = _wa_scores(q4, slab_ref[0:CTX_LEN, WA_KCOL])
    _wa_softmax_pv(sink_ref, [s_c], [slab_ref[0:CTX_LEN, WA_VCOL]], o_ref, blk * WA_BLOCK)


GLA_LEVELS = int(math.log2(GLA_CHUNK))
GLA_ROW_LEVELS = 3
E_ROWS = (1 + GLA_LEVELS - GLA_ROW_LEVELS) * GLA_CHUNK


def _gla_constants():
    C = GLA_CHUNK
    p = np.arange(C)
    r = p[None, :]
    mats = [r <= p[:, None]]
    masks = []
    for lvl in range(GLA_LEVELS):
        m = C >> (lvl + 1)
        pair = p // (2 * m)
        half = (p // m) % 2
        a_end = pair * 2 * m + m - 1
        if lvl >= GLA_ROW_LEVELS:
            mats.append(r <= a_end[:, None])
        masks.append((pair[:, None] == pair[None, :]) & (half[:, None] == 1) & (half[None, :] == 0))
    masks.append(p[:, None] == p[None, :])
    fwd = np.concatenate(mats, axis=0).astype(np.float32)
    bwd = np.concatenate([blk[::-1, ::-1] for blk in mats], axis=0).astype(np.float32)
    msk = np.stack(masks).astype(np.float32)
    lmask = np.stack([msk, msk[:, ::-1, ::-1]])
    lmask = np.concatenate([lmask, lmask], axis=-1)
    return np.stack([fwd, bwd]), lmask


def _gla_tiles(emat_ref, lmask_ref, qkf_ref, laf_ref, vf_ref, qkb_ref, lab_ref, vb_ref,
               of_ref, ob_ref, st_ref):
    C = GLA_CHUNK
    lo = _lane_lo((C, LANES))
    v_lo = _lane_lo((C, 2 * GLA_DV), width=2 * GLA_DV)
    st_rows = lax.broadcasted_iota(jnp.int32, (2 * GLA_DV, LANES), 0) < GLA_DV
    st_keep = st_rows == _lane_lo((2 * GLA_DV, LANES))

    def head_block(kk):
        zero = jnp.zeros_like(kk)
        return jnp.concatenate([jnp.where(lo, kk, zero), jnp.where(lo, zero, kk)], axis=0)

    def chunk(bb, d, p, r0, qk_ref, la_ref, v_ref, o_ref):
        rows = slice(r0, r0 + C)
        g = la_ref[bb, rows, p * LANES:(p + 1) * LANES]
        q = qk_ref[bb, rows, p * LANES:(p + 1) * LANES]
        k = qk_ref[bb, rows, GLA_QK_WIDTH + p * LANES:GLA_QK_WIDTH + (p + 1) * LANES]
        v = v_ref[bb, rows, p * 2 * GLA_DV:(p + 1) * 2 * GLA_DV]
        g_hi = g.astype(BF16)
        g_lo = (g - g_hi.astype(F32)).astype(BF16)
        e2 = jnp.dot(emat_ref[d], jnp.concatenate([g_hi, g_lo], axis=1), preferred_element_type=F32)
        e = e2[:, 0:LANES] + e2[:, LANES:2 * LANES]
        b = e[0:C]
        end_row = C - 1 if d == 0 else 0
        b_end = b[end_row:end_row + 1]
        st = st_ref[bb, d, p]
        o = lax.dot_general((q * jnp.exp(b)).astype(BF16), st.astype(BF16), NT_DIMS,
                            preferred_element_type=F32)
        att = lax.dot_general(q.astype(BF16), head_block(k.astype(BF16)), NT_DIMS,
                              preferred_element_type=F32) * lmask_ref[d, GLA_LEVELS]
        for lvl in range(GLA_LEVELS):
            m = C >> (lvl + 1)
            if lvl < GLA_ROW_LEVELS:
                pieces = []
                for u in range(C // (2 * m)):
                    row = 2 * m * u + (m - 1 if d == 0 else m)
                    pieces.append(jnp.broadcast_to(b[row:row + 1], (2 * m, LANES)))
                rho = pieces[0] if len(pieces) == 1 else jnp.concatenate(pieces, axis=0)
            else:
                rho = e[(lvl - GLA_ROW_LEVELS + 1) * C:(lvl - GLA_ROW_LEVELS + 2) * C]
            x = jnp.exp(-jnp.abs(b - rho))
            att = att + lax.dot_general((q * x).astype(BF16), head_block((k * x).astype(BF16)), NT_DIMS,
                                        preferred_element_type=F32) * lmask_ref[d, lvl]
        vzero = jnp.zeros_like(v)
        v_bd = jnp.concatenate([jnp.where(v_lo, v, vzero), jnp.where(v_lo, vzero, v)], axis=0)
        o = o + jnp.dot(att.astype(BF16), v_bd, preferred_element_type=F32)
        o_ref[bb, rows, p * 2 * GLA_DV:(p + 1) * 2 * GLA_DV] = o.astype(BF16)
        ks = (k * jnp.exp(b_end - b)).astype(BF16)
        upd = lax.dot_general(v, ks, TN_DIMS, preferred_element_type=F32)
        st_ref[bb, d, p] = st * jnp.exp(b_end) + jnp.where(st_keep, upd, 0.0)

    n_chunks = TILE // C
    for step in range(n_chunks):
        for bb in range(BP):
            for p in range(GLA_HEADS // 2):
                chunk(bb, 0, p, step * C, qkf_ref, laf_ref, vf_ref, of_ref)
                chunk(bb, 1, p, (n_chunks - 1 - step) * C, qkb_ref, lab_ref, vb_ref, ob_ref)


def _mixer_kernel(sink_ref, na_ref, wa_ref, bias_ref, emat_ref, lmask_ref, qkf_ref, laf_ref, vf_ref,
                  qkb_ref, lab_ref, vb_ref, ona_ref, owa_ref, of_ref, ob_ref, st_ref, *, with_ctx):
    i = pl.program_id(1)
    gla = functools.partial(_gla_tiles, emat_ref, lmask_ref, qkf_ref, laf_ref, vf_ref, qkb_ref, lab_ref, vb_ref,
                            of_ref, ob_ref, st_ref)

    @pl.when(i == 0)
    def _():
        st_ref[...] = jnp.zeros_like(st_ref)
        gla()
        for bb in range(BP):
            if with_ctx:
                _na_ctx(na_ref.at[bb], ona_ref.at[bb])
                for blk in range(CTX_LEN // WA_BLOCK):
                    _wa_ctx(blk, sink_ref, wa_ref.at[bb], owa_ref.at[bb])
            else:
                ona_ref[bb] = jnp.zeros((TILE, NA_WIDTH), BF16)
                owa_ref[bb] = jnp.zeros((TILE, WA_Q_WIDTH), BF16)

    @pl.when(i > 0)
    def _():
        gla()
        for bb in range(BP):
            _na_rows(i - 1, na_ref.at[bb], bias_ref, ona_ref.at[bb])
            for qb in range(WA_PER_STEP):
                _wa_block((i - 1) * WA_PER_STEP + qb, qb * WA_BLOCK, sink_ref, wa_ref.at[bb], owa_ref.at[bb])


def _mixer_call(na, wa, bias, sink, gqk, la, gv, emat, lmask, with_ctx):
    B = gqk.shape[0]
    fwd = lambda b, i: (b, i, 0)
    bwd = lambda b, i: (b, jnp.where(i == 0, 0, N_TILES - i), 0)
    bwd_la = lambda b, i: (b, jnp.where(i == 0, 0, N_TILES - i), 1)
    slab = lambda w: pl.BlockSpec((BP, TOK, w), lambda b, i: (b, 0, 0))
    return pl.pallas_call(
        functools.partial(_mixer_kernel, with_ctx=with_ctx),
        grid=(B // BP, N_TILES),
        in_specs=[
            pl.BlockSpec(memory_space=pltpu.SMEM),
            slab(768), slab(512),
            pl.BlockSpec((NA_HEADS, N_BIAS, GRID_W, LANES), lambda b, i: (0, 0, 0, 0)),
            pl.BlockSpec((2, E_ROWS, GLA_CHUNK), lambda b, i: (0, 0, 0)),
            pl.BlockSpec((2, GLA_LEVELS + 1, GLA_CHUNK, LANES), lambda b, i: (0, 0, 0, 0)),
            pl.BlockSpec((BP, TILE, 512), fwd),
            pl.BlockSpec((BP, TILE, 256), fwd),
            pl.BlockSpec((BP, TILE, 512), fwd),
            pl.BlockSpec((BP, TILE, 512), bwd),
            pl.BlockSpec((BP, TILE, 256), bwd_la),
            pl.BlockSpec((BP, TILE, 512), bwd),
        ],
        out_specs=[pl.BlockSpec((BP, TILE, NA_WIDTH), fwd), pl.BlockSpec((BP, TILE, WA_Q_WIDTH), fwd),
                   pl.BlockSpec((BP, TILE, 512), fwd), pl.BlockSpec((BP, TILE, 512), bwd)],
        out_shape=[jax.ShapeDtypeStruct((B, TOK, NA_WIDTH), BF16), jax.ShapeDtypeStruct((B, TOK, WA_Q_WIDTH), BF16),
                   jax.ShapeDtypeStruct((B, TOK, 512), BF16), jax.ShapeDtypeStruct((B, TOK, 512), BF16)],
        scratch_shapes=[pltpu.VMEM((BP, 2, GLA_HEADS // 2, 2 * GLA_DV, LANES), F32)],
        compiler_params=_cparams(("arbitrary", "arbitrary")),
        name="token_mixers",
    )(sink, na, wa, bias, emat, lmask, gqk, la, gv, gqk, la, gv)


def _outproj_kernel(xc_ref, xl_ref, mod_ref, na_ref, wa_ref, of_ref, ob_ref, go_ref, gn_ref, w_ref, o_ref,
                    *, two_source):
    x = _pick_tile(xc_ref, xl_ref) if two_source else xl_ref[...]
    og = _rows2d(of_ref[...].astype(F32) + ob_ref[...].astype(F32))
    gn = gn_ref[...]
    parts = []
    for h in range(GLA_HEADS):
        seg = og[:, h * GLA_DV:(h + 1) * GLA_DV]
        ms = jnp.mean(seg * seg, axis=-1, keepdims=True)
        parts.append(seg * lax.rsqrt(ms + EPS) * gn)
    go = _rows2d(go_ref[...].astype(F32))
    y_g = (jnp.concatenate(parts, axis=1) * (go * jax.nn.sigmoid(go))).astype(BF16)
    res = (jnp.dot(_rows2d(na_ref[...]), w_ref[0:256, :], preferred_element_type=F32)
           + jnp.dot(_rows2d(wa_ref[...]), w_ref[256:512, :], preferred_element_type=F32)
           + jnp.dot(y_g, w_ref[512:1024, :], preferred_element_type=F32))
    gate = mod_ref[...][:, :, 2 * D_MODEL:3 * D_MODEL]
    o_ref[...] = x + gate * res.reshape(BP, TILE, D_MODEL)


def _outproj_call(x_first, x_rest, rest_tile0, mod_rows, o_na, o_wa, o_f, o_b, go, gn, w_out, with_ctx):
    B = x_first.shape[0]
    t0 = 0 if with_ctx else 1
    n_t = N_TILES - t0
    full = lambda w: pl.BlockSpec((BP, TILE, w), lambda b, i: (b, i + t0, 0))
    own = lambda w: pl.BlockSpec((BP, TILE, w), lambda b, i: (b, i, 0))
    if with_ctx:
        spec_c, spec_l = _two_source_specs(rest_tile0)
    else:
        spec_c = spec_l = full(D_MODEL)
    return pl.pallas_call(
        functools.partial(_outproj_kernel, two_source=with_ctx),
        grid=(B // BP, n_t),
        in_specs=[
            spec_c, spec_l,
            _mod_spec(t0),
            full(NA_WIDTH), full(WA_Q_WIDTH), full(512), full(512), full(512),
            pl.BlockSpec((1, GLA_DV), lambda b, i: (0, 0)),
            pl.BlockSpec((MIX_WIDTH, D_MODEL), lambda b, i: (0, 0)),
        ],
        out_specs=own(D_MODEL),
        out_shape=jax.ShapeDtypeStruct((B, n_t * TILE, D_MODEL), F32),
        compiler_params=_cparams(("arbitrary", "arbitrary")),
        name="out_proj",
    )(x_first, x_rest, mod_rows, o_na, o_wa, o_f, o_b, go, gn, w_out)


HALO = SUBLANES


def _ffn_kernel(x_ref, xp_ref, xn_ref, mod_ref, g2_ref, wv_ref, wg_ref, cw_ref, cb_ref, wd_ref,
                fg_ref, o_ref, gs_ref, *, t0, final):
    ti = pl.program_id(1) + t0
    mod = mod_ref[...]
    shift, scale = mod[:, :, 3 * D_MODEL:4 * D_MODEL], mod[:, :, 4 * D_MODEL:5 * D_MODEL]
    g2 = g2_ref[...]
    x = x_ref[...]
    h_t = _mod_norm(x, g2, shift, scale)
    h_ext = jnp.concatenate([_mod_norm(xp_ref[...], g2, shift, scale), h_t,
                             _mod_norm(xn_ref[...], g2, shift, scale)], axis=1)
    h_ext = _rows2d(h_ext).astype(BF16)
    h_b = _rows2d(h_t).astype(BF16)
    ext = TILE + 2 * HALO
    seq_start = ti <= 1
    seq_end = (ti == 0) | (ti == N_TILES - 1)
    acc = jnp.zeros((BP * TILE, D_MODEL), F32)
    for c in range(FF_DIM // FF_CHUNK):
        cols = slice(c * FF_CHUNK, (c + 1) * FF_CHUNK)
        g_ext = jnp.dot(h_ext, wg_ref[:, cols], preferred_element_type=F32).reshape(BP, ext, FF_CHUNK)
        gs_ref[...] = g_ext
        gs_ref[:, 0:HALO, :] = jnp.where(seq_start, 0.0, g_ext[:, 0:HALO])
        gs_ref[:, HALO + TILE:, :] = jnp.where(seq_end, 0.0, g_ext[:, HALO + TILE:])
        cw = cw_ref[:, cols]
        gate = (cw[0:1] * gs_ref[:, pl.ds(HALO - 1, TILE), :] + cw[1:2] * gs_ref[:, pl.ds(HALO, TILE), :]
                + cw[2:3] * gs_ref[:, pl.ds(HALO + 1, TILE), :] + cb_ref[:, cols])
        gate = _rows2d(gate)
        val = jnp.dot(h_b, wv_ref[:, cols], preferred_element_type=F32)
        act = 0.5 * gate * (1.0 + lax.erf(gate * (1.0 / math.sqrt(2.0))))
        acc = acc + jnp.dot((act * val).astype(BF16), wd_ref[cols, :], preferred_element_type=F32)
    y = x + mod[:, :, 5 * D_MODEL:6 * D_MODEL] * acc.reshape(BP, TILE, D_MODEL)
    if final:
        ms = jnp.mean(y * y, axis=-1, keepdims=True)
        y = y * lax.rsqrt(ms + EPS) * fg_ref[...]
    o_ref[...] = y


def _ffn_call(x1, mod_rows, g2, w_up, cw, cb, wd, fg, with_ctx, final):
    B = x1.shape[0]
    t0 = 0 if with_ctx else 1
    n_t = x1.shape[1] // TILE
    per8 = TILE // HALO
    last8 = x1.shape[1] // HALO - 1
    const = lambda r, c: pl.BlockSpec((r, c), lambda b, i: (0, 0))
    once = pl.Buffered(1)
    return pl.pallas_call(
        functools.partial(_ffn_kernel, t0=t0, final=final),
        grid=(B // BP, n_t),
        in_specs=[
            pl.BlockSpec((BP, TILE, D_MODEL), lambda b, i: (b, i, 0)),
            pl.BlockSpec((BP, HALO, D_MODEL), lambda b, i: (b, jnp.maximum(i * per8 - 1, 0), 0)),
            pl.BlockSpec((BP, HALO, D_MODEL), lambda b, i: (b, jnp.minimum((i + 1) * per8, last8), 0)),
            _mod_spec(t0),
            const(1, D_MODEL),
            pl.BlockSpec((D_MODEL, FF_DIM), lambda b, i: (0, 0), pipeline_mode=once),
            pl.BlockSpec((D_MODEL, FF_DIM), lambda b, i: (0, 1), pipeline_mode=once),
            const(3, FF_DIM), const(1, FF_DIM),
            pl.BlockSpec((FF_DIM, D_MODEL), lambda b, i: (0, 0), pipeline_mode=once),
            const(1, D_MODEL),
        ],
        out_specs=pl.BlockSpec((BP, TILE, D_MODEL), lambda b, i: (b, i, 0)),
        out_shape=jax.ShapeDtypeStruct(x1.shape, F32),
        scratch_shapes=[pltpu.VMEM((BP, TILE + 2 * HALO, FF_CHUNK), F32)],
        compiler_params=_cparams(("arbitrary", "arbitrary")),
        name="conv_ffn",
    )(x1, x1, x1, mod_rows, g2, w_up, w_up, cw, cb, wd, fg)


def _ext_rows(prev_ref, tile, next_ref):
    return jnp.concatenate([prev_ref[...].astype(F32), tile.astype(F32), next_ref[...].astype(F32)], axis=1)


def _tail_kernel(xc_ref, xl_ref, xp_ref, xn_ref, mod_ref,
                 na_ref, nap_ref, nan_ref, wa_ref, wap_ref, wan_ref, of_ref, ofp_ref, ofn_ref,
                 ob_ref, obp_ref, obn_ref, go_ref, gop_ref, gon_ref,
                 gn_ref, wo_ref, g2_ref, wv_ref, wg_ref, cw_ref, cb_ref, wd_ref, fg_ref,
                 o_ref, gs_ref, *, t0, two_source, final):
    ti = pl.program_id(1) + t0
    ext = TILE + 2 * HALO
    mod = mod_ref[...]
    x_t = _pick_tile(xc_ref, xl_ref) if two_source else xl_ref[...]
    x_ext = _ext_rows(xp_ref, x_t, xn_ref)

    og = _rows2d(_ext_rows(ofp_ref, of_ref[...], ofn_ref) + _ext_rows(obp_ref, ob_ref[...], obn_ref))
    gn = gn_ref[...]
    parts = []
    for h in range(GLA_HEADS):
        seg = og[:, h * GLA_DV:(h + 1) * GLA_DV]
        ms = jnp.mean(seg * seg, axis=-1, keepdims=True)
        parts.append(seg * lax.rsqrt(ms + EPS) * gn)
    go = _rows2d(_ext_rows(gop_ref, go_ref[...], gon_ref))
    y_g = (jnp.concatenate(parts, axis=1) * (go * jax.nn.sigmoid(go))).astype(BF16)
    y_na = _rows2d(_ext_rows(nap_ref, na_ref[...], nan_ref)).astype(BF16)
    y_wa = _rows2d(_ext_rows(wap_ref, wa_ref[...], wan_ref)).astype(BF16)
    res = (jnp.dot(y_na, wo_ref[0:256, :], preferred_element_type=F32)
           + jnp.dot(y_wa, wo_ref[256:512, :], preferred_element_type=F32)
           + jnp.dot(y_g, wo_ref[512:1024, :], preferred_element_type=F32))
    x1_ext = x_ext + mod[:, :, 2 * D_MODEL:3 * D_MODEL] * res.reshape(BP, ext, D_MODEL)

    shift, scale = mod[:, :, 3 * D_MODEL:4 * D_MODEL], mod[:, :, 4 * D_MODEL:5 * D_MODEL]
    h_all = _mod_norm(x1_ext, g2_ref[...], shift, scale)
    h_ext = _rows2d(h_all).astype(BF16)
    h_b = _rows2d(h_all[:, HALO:HALO + TILE]).astype(BF16)
    seq_start = ti <= 1
    seq_end = (ti == 0) | (ti == N_TILES - 1)
    acc = jnp.zeros((BP * TILE, D_MODEL), F32)
    for c in range(FF_DIM // FF_CHUNK):
        cols = slice(c * FF_CHUNK, (c + 1) * FF_CHUNK)
        g_ext = jnp.dot(h_ext, wg_ref[:, cols], preferred_element_type=F32).reshape(BP, ext, FF_CHUNK)
        gs_ref[...] = g_ext
        gs_ref[:, 0:HALO, :] = jnp.where(seq_start, 0.0, g_ext[:, 0:HALO])
        gs_ref[:, HALO + TILE:, :] = jnp.where(seq_end, 0.0, g_ext[:, HALO + TILE:])
        cw = cw_ref[:, cols]
        gate = (cw[0:1] * gs_ref[:, pl.ds(HALO - 1, TILE), :] + cw[1:2] * gs_ref[:, pl.ds(HALO, TILE), :]
                + cw[2:3] * gs_ref[:, pl.ds(HALO + 1, TILE), :] + cb_ref[:, cols])
        gate = _rows2d(gate)
        val = jnp.dot(h_b, wv_ref[:, cols], preferred_element_type=F32)
        act = 0.5 * gate * (1.0 + lax.erf(gate * (1.0 / math.sqrt(2.0))))
        acc = acc + jnp.dot((act * val).astype(BF16), wd_ref[cols, :], preferred_element_type=F32)
    y = x1_ext[:, HALO:HALO + TILE] + mod[:, :, 5 * D_MODEL:6 * D_MODEL] * acc.reshape(BP, TILE, D_MODEL)
    if final:
        ms = jnp.mean(y * y, axis=-1, keepdims=True)
        y = y * lax.rsqrt(ms + EPS) * fg_ref[...]
    o_ref[...] = y


def _tail_call(x_first, x_rest, rest_tile0, mod_rows, o_na, o_wa, o_f, o_b, go, gn, w_out,
               g2, w_up, cw, cb, wd, fg, with_ctx, final):
    B = x_first.shape[0]
    t0 = 0 if with_ctx else 1
    n_t = N_TILES - t0
    per8 = TILE // HALO

    def tile_and_halos(w):
        last8 = TOK // HALO - 1
        return [pl.BlockSpec((BP, TILE, w), lambda b, i: (b, i + t0, 0)),
                pl.BlockSpec((BP, HALO, w), lambda b, i: (b, jnp.maximum((i + t0) * per8 - 1, 0), 0)),
                pl.BlockSpec((BP, HALO, w), lambda b, i: (b, jnp.minimum((i + t0 + 1) * per8, last8), 0))]

    if with_ctx:
        spec_c, spec_l = _two_source_specs(rest_tile0)
    else:
        spec_c = spec_l = pl.BlockSpec((BP, TILE, D_MODEL), lambda b, i: (b, i + t0, 0))
    rest_last8 = x_rest.shape[1] // HALO - 1
    r0 = rest_tile0 - 1 + t0
    x_prev = pl.BlockSpec((BP, HALO, D_MODEL), lambda b, i: (b, jnp.clip((i + r0) * per8 - 1, 0, rest_last8), 0))
    x_next = pl.BlockSpec((BP, HALO, D_MODEL), lambda b, i: (b, jnp.clip((i + r0 + 1) * per8, 0, rest_last8), 0))
    const = lambda r, c: pl.BlockSpec((r, c), lambda b, i: (0, 0))
    once = pl.Buffered(1)
    return pl.pallas_call(
        functools.partial(_tail_kernel, t0=t0, two_source=with_ctx, final=final),
        grid=(B // BP, n_t),
        in_specs=[
            spec_c, spec_l, x_prev, x_next, _mod_spec(t0),
            *tile_and_halos(NA_WIDTH), *tile_and_halos(WA_Q_WIDTH),
            *tile_and_halos(512), *tile_and_halos(512), *tile_and_halos(512),
            const(1, GLA_DV),
            pl.BlockSpec((MIX_WIDTH, D_MODEL), lambda b, i: (0, 0), pipeline_mode=once),
            const(1, D_MODEL),
            pl.BlockSpec((D_MODEL, FF_DIM), lambda b, i: (0, 0), pipeline_mode=once),
            pl.BlockSpec((D_MODEL, FF_DIM), lambda b, i: (0, 1), pipeline_mode=once),
            const(3, FF_DIM), const(1, FF_DIM),
            pl.BlockSpec((FF_DIM, D_MODEL), lambda b, i: (0, 0), pipeline_mode=once),
            const(1, D_MODEL),
        ],
        out_specs=pl.BlockSpec((BP, TILE, D_MODEL), lambda b, i: (b, i, 0)),
        out_shape=jax.ShapeDtypeStruct((B, n_t * TILE, D_MODEL), F32),
        scratch_shapes=[pltpu.VMEM((BP, TILE + 2 * HALO, FF_CHUNK), F32)],
        compiler_params=_cparams(("arbitrary", "arbitrary")),
        name="layer_tail",
    )(x_first, x_rest, x_rest, x_rest, mod_rows,
      o_na, o_na, o_na, o_wa, o_wa, o_wa, o_f, o_f, o_f, o_b, o_b, o_b, go, go, go,
      gn, w_out, g2, w_up, w_up, cw, cb, wd, fg)


def _rope_tables():
    t = np.arange(SEQ)
    n_freq = WA_HEAD_DIM // 4
    inv = ROPE_THETA ** (-np.arange(n_freq, dtype=np.float32) / n_freq)
    ang = np.concatenate([(t // GRID_W).astype(np.float32)[:, None] * inv,
                          (t % GRID_W).astype(np.float32)[:, None] * inv], axis=-1)
    cos = np.tile(np.cos(ang), (1, 2 * WA_HEADS))
    sin = np.tile(np.sin(ang), (1, 2 * WA_HEADS))
    cos = np.concatenate([np.ones((CTX_LEN, 256), np.float32), cos], axis=0)
    sin = np.concatenate([np.zeros((CTX_LEN, 256), np.float32), sin], axis=0)
    return jnp.asarray(cos, F32), jnp.asarray(sin, F32)


def _rotate_half_cols(w, n_heads):
    w = w.reshape(w.shape[0], n_heads, 2, WA_HEAD_DIM // 2)
    return jnp.stack([-w[:, :, 1], w[:, :, 0]], axis=2).reshape(w.shape[0], n_heads * WA_HEAD_DIM)


def _layer_weights(w_in, gate_w, gate_b, w_out):
    waq = w_in[:, 3 * NA_WIDTH:3 * NA_WIDTH + WA_Q_WIDTH].reshape(D_MODEL, WA_HEADS, WA_HEAD_DIM)
    waq = waq[:, np.array(WA_HEAD_ORDER)].reshape(D_MODEL, WA_Q_WIDTH)
    wak = w_in[:, W_WAK:W_WAV]
    lr_pad = jnp.zeros((D_MODEL, X_COLS - X_LR - 2 * GLA_GATE_RANK), w_in.dtype)
    w_extra = jnp.concatenate([waq, _rotate_half_cols(waq, WA_HEADS), _rotate_half_cols(wak, WA_KV_HEADS),
                               w_in[:, W_LR:], lr_pad], axis=1).astype(BF16)
    zero = jnp.zeros((GLA_GATE_RANK, GLA_QK_WIDTH), F32)
    gw = jnp.concatenate([
        jnp.concatenate([gate_w[0], zero], axis=1),
        jnp.concatenate([zero, gate_w[1]], axis=1),
        jnp.zeros((LANES - 2 * GLA_GATE_RANK, 2 * GLA_QK_WIDTH), F32)], axis=0).astype(BF16)
    gb = gate_b.reshape(1, 2 * GLA_QK_WIDTH)
    wo_wa = w_out[NA_WIDTH:NA_WIDTH + WA_Q_WIDTH].reshape(WA_HEADS, WA_HEAD_DIM, D_MODEL)[np.array(WA_HEAD_ORDER)]
    wo = jnp.concatenate([w_out[:NA_WIDTH], wo_wa.reshape(WA_Q_WIDTH, D_MODEL),
                          w_out[NA_WIDTH + WA_Q_WIDTH:]], axis=0).astype(BF16)
    return w_in.astype(BF16), w_extra, gw, gb, wo


def kernel(x, c, ctx, c_ctx, w_mod, b_mod, norm1_g, norm2_g, w_in, na_rpb, wa_sink, gla_gate_w,
           gla_gate_b, gla_norm_g, w_out, ffn_w_up, ffn_conv_w, ffn_conv_b, ffn_w_down, final_norm_g):
    B = x.shape[0]
    assert x.shape == (B, SEQ, D_MODEL) and ctx.shape == (B, CTX_LEN, D_MODEL) and B % BP == 0
    n_c = -(-(B + 1) // SUBLANES) * SUBLANES
    cc = jnp.concatenate([c, c_ctx[None], jnp.zeros((n_c - B - 1, D_MODEL), F32)], axis=0)
    mods = _mod_call(cc, w_mod, b_mod)
    cos_t, sin_t = _rope_tables()
    emat, lmask = _gla_constants()
    emat = jnp.asarray(emat, BF16)
    lmask = jnp.asarray(lmask, F32)
    fg = final_norm_g.reshape(1, D_MODEL)

    x_first, x_rest, rest_tile0 = ctx, x, 0
    out = None
    for l in range(DEPTH):
        last = l == DEPTH - 1
        mod_rows = jnp.stack([jnp.broadcast_to(mods[l, B], (B, 6 * D_MODEL)), mods[l, :B]],
                             axis=0).reshape(2, B, 1, 6 * D_MODEL)
        w_main, w_extra, gw, gb, wo = _layer_weights(w_in[l], gla_gate_w[l], gla_gate_b[l], w_out[l])
        na, wa, gqk, gv, go, la = _inproj_call(x_first, x_rest, rest_tile0, mod_rows,
                                               norm1_g[l].reshape(1, D_MODEL), w_main, w_extra, gw, gb, cos_t, sin_t)
        bias = _nabias_call(na_rpb[l])
        o_na, o_wa, o_f, o_b = _mixer_call(na, wa, bias, wa_sink[l], gqk, la, gv, emat, lmask, with_ctx=not last)
        x2 = _tail_call(x_first, x_rest, rest_tile0, mod_rows, o_na, o_wa, o_f, o_b, go,
                        gla_norm_g[l].reshape(1, GLA_DV), wo, norm2_g[l].reshape(1, D_MODEL),
                        ffn_w_up[l].astype(BF16), ffn_conv_w[l], ffn_conv_b[l].reshape(1, FF_DIM),
                        ffn_w_down[l].astype(BF16), fg, with_ctx=not last, final=last)
        if last:
            out = x2
        else:
            x_first, x_rest, rest_tile0 = x2, x2, 1
    return out
```

```python
import functools
import math

import numpy as np
import jax
import jax.numpy as jnp
from jax import lax
from jax.experimental import pallas as pl
from jax.experimental.pallas import tpu as pltpu

D_MODEL = 1024
SEQ = 2048
DEPTH = 2
CTX_LEN = 256
GRID_W = 64
GRID_ROWS = SEQ // GRID_W
TOK = CTX_LEN + SEQ

NA_HEADS = 4
NA_HEAD_DIM = 64
NA_WIN_R = 8
NA_WIN_C = 16
WA_HEADS = 4
WA_KV_HEADS = 2
WA_HEAD_DIM = 64
WA_WINDOW = 128
WA_BLOCK = 128
GLA_HEADS = 4
GLA_DK = 64
GLA_DV = 128
GLA_GATE_RANK = 16
GLA_GATE_TAU = 16.0
GLA_CHUNK = 64

NA_WIDTH = NA_HEADS * NA_HEAD_DIM
WA_Q_WIDTH = WA_HEADS * WA_HEAD_DIM
WA_KV_WIDTH = WA_KV_HEADS * WA_HEAD_DIM
GLA_QK_WIDTH = GLA_HEADS * GLA_DK
GLA_V_WIDTH = GLA_HEADS * GLA_DV
MIX_WIDTH = NA_WIDTH + WA_Q_WIDTH + GLA_V_WIDTH
IN_WIDTH = 3 * NA_WIDTH + WA_Q_WIDTH + 2 * WA_KV_WIDTH + 2 * GLA_QK_WIDTH + 2 * GLA_V_WIDTH + 2 * GLA_GATE_RANK

FF_DIM = 2816
FF_CHUNK = 1408
ROPE_THETA = 10000.0
EPS = 1e-6

BP = 2
TILE = 256
N_TILES = TOK // TILE
LANES = 128
SUBLANES = 8
NEG = -1e30
VMEM_LIMIT = 56 * 1024 * 1024

F32 = jnp.float32
BF16 = jnp.bfloat16
NT_DIMS = (((1,), (1,)), ((), ()))
TN_DIMS = (((0,), (0,)), ((), ()))

W_NA = 0
W_WAK = 1024
W_WAV = 1152
W_GQ = 1280
W_GK = 1536
W_GV = 1792
W_GO = 2304
W_LR = 2816
X_WAQ = 0
X_LR = 256
X_COLS = 384

WA_HEAD_ORDER = (0, 2, 1, 3)
Q_SCALE = NA_HEAD_DIM ** -0.5


def _cparams(sem):
    return pltpu.CompilerParams(dimension_semantics=sem, vmem_limit_bytes=VMEM_LIMIT)


def _lane_lo(shape, width=LANES):
    lane = lax.broadcasted_iota(jnp.int32, shape, len(shape) - 1)
    return (lane & (width - 1)) < (width // 2)


def _mod_kernel(c_ref, w_ref, b_ref, o_ref):
    c = c_ref[...]
    sc = c * jax.nn.sigmoid(c)
    o_ref[...] = jnp.dot(sc.astype(BF16), w_ref[...].astype(BF16),
                         preferred_element_type=F32) + b_ref[...]


def _mod_call(cc, w_mod, b_mod):
    tn = 1536
    rows = cc.shape[0]
    return pl.pallas_call(
        _mod_kernel,
        grid=(DEPTH, 6 * D_MODEL // tn),
        in_specs=[
            pl.BlockSpec((rows, D_MODEL), lambda l, j: (0, 0)),
            pl.BlockSpec((None, D_MODEL, tn), lambda l, j: (l, 0, j)),
            pl.BlockSpec((None, 1, tn), lambda l, j: (l, 0, j)),
        ],
        out_specs=pl.BlockSpec((None, rows, tn), lambda l, j: (l, 0, j)),
        out_shape=jax.ShapeDtypeStruct((DEPTH, rows, 6 * D_MODEL), F32),
        compiler_params=_cparams(("arbitrary", "arbitrary")),
        name="adaln_mod",
    )(cc, w_mod, b_mod.reshape(DEPTH, 1, 6 * D_MODEL))


N_RO = 2 * NA_WIN_R - 1
N_CO = 2 * NA_WIN_C - 1
N_BIAS = N_RO + 1


def _nabias_kernel(rpb_ref, o_ref):
    h = pl.program_id(0)
    shape = (GRID_W, LANES)
    q = lax.broadcasted_iota(jnp.int32, shape, 0)
    kc = lax.broadcasted_iota(jnp.int32, shape, 1) & (GRID_W - 1)
    d = kc - q + (NA_WIN_C - 1)
    cs = jnp.clip(q - NA_WIN_C // 2, 0, GRID_W - NA_WIN_C)
    inwin = (kc >= cs) & (kc < cs + NA_WIN_C)
    base = h * (N_RO * N_CO)
    o_ref[0] = jnp.full(shape, NEG, F32)
    for ro in range(N_RO):
        acc = jnp.full(shape, NEG, F32)
        for dd in range(N_CO):
            acc = jnp.where(d == dd, rpb_ref[base + ro * N_CO + dd], acc)
        o_ref[1 + ro] = jnp.where(inwin, acc, NEG)


def _nabias_call(rpb):
    return pl.pallas_call(
        _nabias_kernel,
        grid=(NA_HEADS,),
        in_specs=[pl.BlockSpec(memory_space=pltpu.SMEM)],
        out_specs=pl.BlockSpec((None, N_BIAS, GRID_W, LANES), lambda h: (h, 0, 0, 0)),
        out_shape=jax.ShapeDtypeStruct((NA_HEADS, N_BIAS, GRID_W, LANES), F32),
        compiler_params=_cparams(("arbitrary",)),
        name="na_bias",
    )(rpb.reshape(-1))


def _mod_norm(x, g, shift, scale):
    ms = jnp.mean(x * x, axis=-1, keepdims=True)
    return (x * lax.rsqrt(ms + EPS) * g) * (1.0 + scale) + shift


def _pick_tile(first_ref, rest_ref):
    return jnp.where(pl.program_id(1) == 0, first_ref[...], rest_ref[...])


def _rows2d(a):
    return a.reshape(a.shape[0] * a.shape[1], a.shape[2])


def _inproj_kernel(xc_ref, xl_ref, mod_ref, g1_ref, w_ref, wx_ref, gw_ref, gb_ref, cos_ref, sin_ref,
                   na_ref, wa_ref, gqk_ref, gv_ref, go_ref, la_ref):
    mod = mod_ref[...]
    h = _mod_norm(_pick_tile(xc_ref, xl_ref), g1_ref[...], mod[:, :, 0:D_MODEL], mod[:, :, D_MODEL:2 * D_MODEL])
    hb = _rows2d(h).astype(BF16)

    def proj(w, c0, c1):
        return jnp.dot(hb, w[:, c0:c1], preferred_element_type=F32).reshape(BP, TILE, c1 - c0)

    na_ref[:, :, 0:NA_WIDTH] = (proj(w_ref, W_NA, W_NA + NA_WIDTH) * Q_SCALE).astype(BF16)
    na_ref[:, :, NA_WIDTH:3 * NA_WIDTH] = proj(w_ref, W_NA + NA_WIDTH, W_NA + 3 * NA_WIDTH).astype(BF16)
    cos = cos_ref[...]
    sin = sin_ref[...]

    def rotate_half(a):
        groups = []
        for g in range(a.shape[-1] // LANES):
            blk = _rows2d(a[:, :, g * LANES:(g + 1) * LANES])
            lane = lax.broadcasted_iota(jnp.int32, blk.shape, 1)
            first_half = (lane & (WA_HEAD_DIM // 2)) == 0
            rot = jnp.where(first_half, -pltpu.roll(blk, LANES - WA_HEAD_DIM // 2, 1),
                            pltpu.roll(blk, WA_HEAD_DIM // 2, 1))
            groups.append(rot.reshape(BP, TILE, LANES))
        return jnp.concatenate(groups, axis=-1)

    wa_q = proj(wx_ref, X_WAQ, X_LR)
    wa_k = proj(w_ref, W_WAK, W_WAV)
    wa_ref[:, :, 0:256] = ((wa_q * cos + rotate_half(wa_q) * sin) * Q_SCALE).astype(BF16)
    wa_ref[:, :, 256:384] = (wa_k * cos[:, 0:LANES] + rotate_half(wa_k) * sin[:, 0:LANES]).astype(BF16)
    wa_ref[:, :, 384:512] = proj(w_ref, W_WAV, W_GQ).astype(BF16)
    gqk_ref[:, :, 0:GLA_QK_WIDTH] = proj(w_ref, W_GQ, W_GK) * Q_SCALE
    gqk_ref[:, :, GLA_QK_WIDTH:] = proj(w_ref, W_GK, W_GV)
    gv_ref[...] = proj(w_ref, W_GV, W_GO).astype(BF16)
    go_ref[...] = proj(w_ref, W_GO, W_LR).astype(BF16)
    lr = _rows2d(proj(wx_ref, X_LR, X_COLS)).astype(BF16)
    logit = jnp.dot(lr, gw_ref[...], preferred_element_type=F32) + gb_ref[...]
    log_sig = jnp.minimum(logit, 0.0) - jnp.log(1.0 + jnp.exp(-jnp.abs(logit)))
    la_ref[...] = (log_sig * (1.0 / GLA_GATE_TAU)).reshape(BP, TILE, 2 * GLA_QK_WIDTH)


def _two_source_specs(rest_tile0):
    return (pl.BlockSpec((BP, TILE, D_MODEL), lambda b, i: (b, 0, 0)),
            pl.BlockSpec((BP, TILE, D_MODEL), lambda b, i: (b, jnp.maximum(i - 1, 0) + rest_tile0, 0)))


def _mod_spec(t0):
    return pl.BlockSpec((None, BP, 1, 6 * D_MODEL), lambda b, i: (jnp.minimum(i + t0, 1), b, 0, 0))


def _layer_spec(layer, r, c, **kw):
    return pl.BlockSpec((None, r, c), lambda b, i: (layer, 0, 0), **kw)


def _inproj_call(layer, x_first, x_rest, rest_tile0, mod_rows, g1, w_main, w_extra, gw, gb, cos_t, sin_t):
    B = x_first.shape[0]
    tile = lambda w: pl.BlockSpec((BP, TILE, w), lambda b, i: (b, i, 0))
    spec_c, spec_l = _two_source_specs(rest_tile0)
    return pl.pallas_call(
        _inproj_kernel,
        grid=(B // BP, N_TILES),
        in_specs=[
            spec_c, spec_l,
            _mod_spec(0),
            _layer_spec(layer, 1, D_MODEL),
            _layer_spec(layer, D_MODEL, IN_WIDTH),
            _layer_spec(layer, D_MODEL, X_COLS),
            _layer_spec(layer, LANES, 2 * GLA_QK_WIDTH),
            _layer_spec(layer, 1, 2 * GLA_QK_WIDTH),
            pl.BlockSpec((TILE, 256), lambda b, i: (i, 0)),
            pl.BlockSpec((TILE, 256), lambda b, i: (i, 0)),
        ],
        out_specs=[tile(768), tile(512), tile(512), tile(512), tile(512), tile(512)],
        out_shape=[
            jax.ShapeDtypeStruct((B, TOK, 768), BF16),
            jax.ShapeDtypeStruct((B, TOK, 512), BF16),
            jax.ShapeDtypeStruct((B, TOK, 512), F32),
            jax.ShapeDtypeStruct((B, TOK, 512), BF16),
            jax.ShapeDtypeStruct((B, TOK, 512), BF16),
            jax.ShapeDtypeStruct((B, TOK, 512), F32),
        ],
        compiler_params=_cparams(("arbitrary", "arbitrary")),
        name="in_proj",
    )(x_first, x_rest, mod_rows, g1, w_main, w_extra, gw, gb, cos_t, sin_t)


NA_ROWS_PER_STEP = TILE // GRID_W
NA_STEPS = GRID_ROWS // NA_ROWS_PER_STEP
NA_UNION = NA_WIN_R + NA_ROWS_PER_STEP


def _split_heads_rows(q):
    lo = _lane_lo(q.shape)
    zero = jnp.zeros_like(q)
    return jnp.concatenate([jnp.where(lo, q, zero), jnp.where(lo, zero, q)], axis=0)


def _merge_heads_rows(o, n):
    return jnp.where(_lane_lo((n, LANES)), o[0:n], o[n:2 * n])


def _na_rows(j, slab_ref, bias_ref, o_ref):
    n_union = NA_UNION * GRID_W
    u0 = jnp.clip(j * NA_ROWS_PER_STEP - NA_WIN_R // 2, 0, GRID_ROWS - NA_UNION)
    qrow = pl.multiple_of(CTX_LEN + j * TILE, TILE)
    krow = pl.multiple_of(CTX_LEN + u0 * GRID_W, GRID_W)
    lo = _lane_lo((GRID_W, LANES))
    tile_idx = []
    for a in range(NA_ROWS_PER_STEP):
        r = j * NA_ROWS_PER_STEP + a
        rs = jnp.clip(r - NA_WIN_R // 2, 0, GRID_ROWS - NA_WIN_R)
        idx = []
        for u in range(NA_UNION):
            key_row = u0 + u
            inside = (key_row >= rs) & (key_row < rs + NA_WIN_R)
            idx.append(jnp.where(inside, key_row - r + NA_WIN_R, 0))
        tile_idx.append(idx)
    for p in range(NA_HEADS // 2):
        kc = slab_ref[0:CTX_LEN, 256 + p * LANES:256 + (p + 1) * LANES]
        vc = slab_ref[0:CTX_LEN, 512 + p * LANES:512 + (p + 1) * LANES]
        ku = slab_ref[pl.ds(krow, n_union), 256 + p * LANES:256 + (p + 1) * LANES]
        vu = slab_ref[pl.ds(krow, n_union), 512 + p * LANES:512 + (p + 1) * LANES]
        q_all = slab_ref[pl.ds(qrow, TILE), p * LANES:(p + 1) * LANES]
        q2 = jnp.concatenate([_split_heads_rows(q_all[a * GRID_W:(a + 1) * GRID_W])
                              for a in range(NA_ROWS_PER_STEP)], axis=0)
        bias = jnp.concatenate([
            jnp.concatenate([jnp.where(lo, bias_ref[2 * p + hh, tile_idx[a][2 * jj]],
                                       bias_ref[2 * p + hh, tile_idx[a][2 * jj + 1]])
                             for jj in range(NA_UNION // 2)], axis=1)
            for a in range(NA_ROWS_PER_STEP) for hh in range(2)], axis=0)
        s_u = lax.dot_general(q2, ku, NT_DIMS, preferred_element_type=F32) + bias
        s_c = lax.dot_general(q2, kc, NT_DIMS, preferred_element_type=F32)
        m = jnp.maximum(jnp.max(s_u, axis=-1, keepdims=True), jnp.max(s_c, axis=-1, keepdims=True))
        p_u = jnp.exp(s_u - m)
        p_c = jnp.exp(s_c - m)
        den = jnp.sum(p_u, axis=-1, keepdims=True) + jnp.sum(p_c, axis=-1, keepdims=True)
        o = (jnp.dot(p_u.astype(BF16), vu, preferred_element_type=F32)
             + jnp.dot(p_c.astype(BF16), vc, preferred_element_type=F32)) / den
        out = [_merge_heads_rows(o[a * 2 * GRID_W:(a + 1) * 2 * GRID_W], GRID_W)
               for a in range(NA_ROWS_PER_STEP)]
        o_ref[:, p * LANES:(p + 1) * LANES] = jnp.concatenate(out, axis=0).astype(BF16)


def _na_ctx(slab_ref, o_ref):
    for p in range(NA_HEADS // 2):
        kc = slab_ref[0:CTX_LEN, 256 + p * LANES:256 + (p + 1) * LANES]
        vc = slab_ref[0:CTX_LEN, 512 + p * LANES:512 + (p + 1) * LANES]
        q2 = _split_heads_rows(slab_ref[0:CTX_LEN, p * LANES:(p + 1) * LANES])
        s_c = lax.dot_general(q2, kc, NT_DIMS, preferred_element_type=F32)
        p_c = jnp.exp(s_c - jnp.max(s_c, axis=-1, keepdims=True))
        den = jnp.sum(p_c, axis=-1, keepdims=True)
        o = jnp.dot(p_c.astype(BF16), vc, preferred_element_type=F32) / den
        o_ref[:, p * LANES:(p + 1) * LANES] = _merge_heads_rows(o, CTX_LEN).astype(BF16)


N_WA_BLOCKS = SEQ // WA_BLOCK
WA_PER_STEP = TILE // WA_BLOCK
WA_KCOL = slice(256, 384)
WA_VCOL = slice(384, 512)


def _wa_stack_heads(slab_ref, qrow):
    lo = _lane_lo((WA_BLOCK, LANES))
    qa = slab_ref[pl.ds(qrow, WA_BLOCK), 0:LANES]
    qb = slab_ref[pl.ds(qrow, WA_BLOCK), LANES:2 * LANES]
    zero = jnp.zeros_like(qa)
    return jnp.concatenate([jnp.where(lo, qa, zero), jnp.where(lo, zero, qa),
                            jnp.where(lo, qb, zero), jnp.where(lo, zero, qb)], axis=0)


def _wa_softmax_pv(sink_ref, scores, values, o_ref, out_row):
    rows4 = WA_HEADS * WA_BLOCK
    slot = lax.broadcasted_iota(jnp.int32, (rows4, 1), 0) // WA_BLOCK
    sink = jnp.zeros((rows4, 1), F32)
    for s_i, h in enumerate(WA_HEAD_ORDER):
        sink = jnp.where(slot == s_i, sink_ref[h], sink)
    m = sink
    for s in scores:
        m = jnp.maximum(m, jnp.max(s, axis=-1, keepdims=True))
    den = jnp.exp(sink - m)
    o = None
    for s, v in zip(scores, values):
        e = jnp.exp(s - m)
        den = den + jnp.sum(e, axis=-1, keepdims=True)
        t = jnp.dot(e.astype(BF16), v, preferred_element_type=F32)
        o = t if o is None else o + t
    o = o / den
    b = WA_BLOCK
    lo = _lane_lo((b, LANES))
    rows = slice(out_row, out_row + b)
    o_ref[rows, 0:LANES] = jnp.where(lo, o[0:b], o[b:2 * b]).astype(BF16)
    o_ref[rows, LANES:2 * LANES] = jnp.where(lo, o[2 * b:3 * b], o[3 * b:4 * b]).astype(BF16)


def _wa_scores(q4, k):
    return lax.dot_general(q4, k, NT_DIMS, preferred_element_type=F32)


def _wa_block(n, out_row, sink_ref, slab_ref, o_ref):
    rows4 = WA_HEADS * WA_BLOCK
    qrow = pl.multiple_of(CTX_LEN + n * WA_BLOCK, WA_BLOCK)
    prow = pl.multiple_of(qrow - WA_BLOCK, WA_BLOCK)
    nrow = pl.multiple_of(jnp.minimum(qrow + WA_BLOCK, TOK - WA_BLOCK), WA_BLOCK)
    k_blocks = [slab_ref[pl.ds(s, WA_BLOCK), WA_KCOL] for s in (prow, qrow, nrow)]
    v_blocks = [slab_ref[pl.ds(s, WA_BLOCK), WA_VCOL] for s in (prow, qrow, nrow)]
    i = lax.broadcasted_iota(jnp.int32, (rows4, WA_BLOCK), 0) & (WA_BLOCK - 1)
    jj = lax.broadcasted_iota(jnp.int32, (rows4, WA_BLOCK), 1)
    keep_prev = (jj >= i) & (n > 0)
    keep_next = (jj <= i) & (n < N_WA_BLOCKS - 1)
    q4 = _wa_stack_heads(slab_ref, qrow)
    s_p = jnp.where(keep_prev, _wa_scores(q4, k_blocks[0]), NEG)
    s_m = _wa_scores(q4, k_blocks[1])
    s_n = jnp.where(keep_next, _wa_scores(q4, k_blocks[2]), NEG)
    s_c = _wa_scores(q4, slab_ref[0:CTX_LEN, WA_KCOL])
    _wa_softmax_pv(sink_ref, [s_p, s_m, s_n, s_c], v_blocks + [slab_ref[0:CTX_LEN, WA_VCOL]], o_ref, out_row)


def _wa_ctx(blk, sink_ref, slab_ref, o_ref):
    q4 = _wa_stack_heads(slab_ref, blk * WA_BLOCK)
    s_c = _wa_scores(q4, slab_ref[0:CTX_LEN, WA_KCOL])
    _wa_softmax_pv(sink_ref, [s_c], [slab_ref[0:CTX_LEN, WA_VCOL]], o_ref, blk * WA_BLOCK)


GLA_LEVELS = int(math.log2(GLA_CHUNK))
GLA_ROW_LEVELS = 3
E_ROWS = (1 + GLA_LEVELS - GLA_ROW_LEVELS) * GLA_CHUNK


def _gla_constants():
    C = GLA_CHUNK
    p = np.arange(C)
    r = p[None, :]
    mats = [r <= p[:, None]]
    masks = []
    for lvl in range(GLA_LEVELS):
        m = C >> (lvl + 1)
        pair = p // (2 * m)
        half = (p // m) % 2
        a_end = pair * 2 * m + m - 1
        if lvl >= GLA_ROW_LEVELS:
            mats.append(r <= a_end[:, None])
        masks.append((pair[:, None] == pair[None, :]) & (half[:, None] == 1) & (half[None, :] == 0))
    masks.append(p[:, None] == p[None, :])
    fwd = np.concatenate(mats, axis=0).astype(np.float32)
    bwd = np.concatenate([blk[::-1, ::-1] for blk in mats], axis=0).astype(np.float32)
    msk = np.stack(masks).astype(np.float32)
    lmask = np.stack([msk, msk[:, ::-1, ::-1]])
    lmask = np.concatenate([lmask, lmask], axis=-1)
    return np.stack([fwd, bwd]), lmask


def _gla_tiles(emat_ref, lmask_ref, qkf_ref, laf_ref, vf_ref, qkb_ref, lab_ref, vb_ref,
               of_ref, ob_ref, st_ref):
    C = GLA_CHUNK
    lo = _lane_lo((C, LANES))
    v_lo = _lane_lo((C, 2 * GLA_DV), width=2 * GLA_DV)
    st_rows = lax.broadcasted_iota(jnp.int32, (2 * GLA_DV, LANES), 0) < GLA_DV
    st_keep = st_rows == _lane_lo((2 * GLA_DV, LANES))

    def head_block(kk):
        zero = jnp.zeros_like(kk)
        return jnp.concatenate([jnp.where(lo, kk, zero), jnp.where(lo, zero, kk)], axis=0)

    def chunk(bb, d, p, r0, qk_ref, la_ref, v_ref, o_ref):
        rows = slice(r0, r0 + C)
        g = la_ref[bb, rows, p * LANES:(p + 1) * LANES]
        q = qk_ref[bb, rows, p * LANES:(p + 1) * LANES]
        k = qk_ref[bb, rows, GLA_QK_WIDTH + p * LANES:GLA_QK_WIDTH + (p + 1) * LANES]
        v = v_ref[bb, rows, p * 2 * GLA_DV:(p + 1) * 2 * GLA_DV]
        g_hi = g.astype(BF16)
        g_lo = (g - g_hi.astype(F32)).astype(BF16)
        e2 = jnp.dot(emat_ref[d], jnp.concatenate([g_hi, g_lo], axis=1), preferred_element_type=F32)
        e = e2[:, 0:LANES] + e2[:, LANES:2 * LANES]
        b = e[0:C]
        end_row = C - 1 if d == 0 else 0
        b_end = b[end_row:end_row + 1]
        st = st_ref[bb, d, p]
        o = lax.dot_general((q * jnp.exp(b)).astype(BF16), st.astype(BF16), NT_DIMS,
                            preferred_element_type=F32)
        att = lax.dot_general(q.astype(BF16), head_block(k.astype(BF16)), NT_DIMS,
                              preferred_element_type=F32) * lmask_ref[d, GLA_LEVELS]
        for lvl in range(GLA_LEVELS):
            m = C >> (lvl + 1)
            if lvl < GLA_ROW_LEVELS:
                pieces = []
                for u in range(C // (2 * m)):
                    row = 2 * m * u + (m - 1 if d == 0 else m)
                    pieces.append(jnp.broadcast_to(b[row:row + 1], (2 * m, LANES)))
                rho = pieces[0] if len(pieces) == 1 else jnp.concatenate(pieces, axis=0)
            else:
                rho = e[(lvl - GLA_ROW_LEVELS + 1) * C:(lvl - GLA_ROW_LEVELS + 2) * C]
            x = jnp.exp(-jnp.abs(b - rho))
            att = att + lax.dot_general((q * x).astype(BF16), head_block((k * x).astype(BF16)), NT_DIMS,
                                        preferred_element_type=F32) * lmask_ref[d, lvl]
        vzero = jnp.zeros_like(v)
        v_bd = jnp.concatenate([jnp.where(v_lo, v, vzero), jnp.where(v_lo, vzero, v)], axis=0)
        o = o + jnp.dot(att.astype(BF16), v_bd, preferred_element_type=F32)
        o_ref[bb, rows, p * 2 * GLA_DV:(p + 1) * 2 * GLA_DV] = o.astype(BF16)
        ks = (k * jnp.exp(b_end - b)).astype(BF16)
        upd = lax.dot_general(v, ks, TN_DIMS, preferred_element_type=F32)
        st_ref[bb, d, p] = st * jnp.exp(b_end) + jnp.where(st_keep, upd, 0.0)

    n_chunks = TILE // C
    for step in range(n_chunks):
        for bb in range(BP):
            for p in range(GLA_HEADS // 2):
                chunk(bb, 0, p, step * C, qkf_ref, laf_ref, vf_ref, of_ref)
                chunk(bb, 1, p, (n_chunks - 1 - step) * C, qkb_ref, lab_ref, vb_ref, ob_ref)


def _mixer_kernel(sink_ref, na_ref, wa_ref, bias_ref, emat_ref, lmask_ref, qkf_ref, laf_ref, vf_ref,
                  qkb_ref, lab_ref, vb_ref, ona_ref, owa_ref, of_ref, ob_ref, st_ref, *, with_ctx):
    i = pl.program_id(1)
    gla = functools.partial(_gla_tiles, emat_ref, lmask_ref, qkf_ref, laf_ref, vf_ref, qkb_ref, lab_ref, vb_ref,
                            of_ref, ob_ref, st_ref)

    @pl.when(i == 0)
    def _():
        st_ref[...] = jnp.zeros_like(st_ref)
        gla()
        for bb in range(BP):
            if with_ctx:
                _na_ctx(na_ref.at[bb], ona_ref.at[bb])
                for blk in range(CTX_LEN // WA_BLOCK):
                    _wa_ctx(blk, sink_ref, wa_ref.at[bb], owa_ref.at[bb])
            else:
                ona_ref[bb] = jnp.zeros((TILE, NA_WIDTH), BF16)
                owa_ref[bb] = jnp.zeros((TILE, WA_Q_WIDTH), BF16)

    @pl.when(i > 0)
    def _():
        gla()
        for bb in range(BP):
            _na_rows(i - 1, na_ref.at[bb], bias_ref, ona_ref.at[bb])
            for qb in range(WA_PER_STEP):
                _wa_block((i - 1) * WA_PER_STEP + qb, qb * WA_BLOCK, sink_ref, wa_ref.at[bb], owa_ref.at[bb])


def _mixer_call(na, wa, bias, sink, gqk, la, gv, emat, lmask, with_ctx):
    B = gqk.shape[0]
    fwd = lambda b, i: (b, i, 0)
    bwd = lambda b, i: (b, jnp.where(i == 0, 0, N_TILES - i), 0)
    bwd_la = lambda b, i: (b, jnp.where(i == 0, 0, N_TILES - i), 1)
    slab = lambda w: pl.BlockSpec((BP, TOK, w), lambda b, i: (b, 0, 0))
    return pl.pallas_call(
        functools.partial(_mixer_kernel, with_ctx=with_ctx),
        grid=(B // BP, N_TILES),
        in_specs=[
            pl.BlockSpec(memory_space=pltpu.SMEM),
            slab(768), slab(512),
            pl.BlockSpec((NA_HEADS, N_BIAS, GRID_W, LANES), lambda b, i: (0, 0, 0, 0)),
            pl.BlockSpec((2, E_ROWS, GLA_CHUNK), lambda b, i: (0, 0, 0)),
            pl.BlockSpec((2, GLA_LEVELS + 1, GLA_CHUNK, LANES), lambda b, i: (0, 0, 0, 0)),
            pl.BlockSpec((BP, TILE, 512), fwd),
            pl.BlockSpec((BP, TILE, 256), fwd),
            pl.BlockSpec((BP, TILE, 512), fwd),
            pl.BlockSpec((BP, TILE, 512), bwd),
            pl.BlockSpec((BP, TILE, 256), bwd_la),
            pl.BlockSpec((BP, TILE, 512), bwd),
        ],
        out_specs=[pl.BlockSpec((BP, TILE, NA_WIDTH), fwd), pl.BlockSpec((BP, TILE, WA_Q_WIDTH), fwd),
                   pl.BlockSpec((BP, TILE, 512), fwd), pl.BlockSpec((BP, TILE, 512), bwd)],
        out_shape=[jax.ShapeDtypeStruct((B, TOK, NA_WIDTH), BF16), jax.ShapeDtypeStruct((B, TOK, WA_Q_WIDTH), BF16),
                   jax.ShapeDtypeStruct((B, TOK, 512), BF16), jax.ShapeDtypeStruct((B, TOK, 512), BF16)],
        scratch_shapes=[pltpu.VMEM((BP, 2, GLA_HEADS // 2, 2 * GLA_DV, LANES), F32)],
        compiler_params=_cparams(("arbitrary", "arbitrary")),
        name="token_mixers",
    )(sink, na, wa, bias, emat, lmask, gqk, la, gv, gqk, la, gv)


HALO = SUBLANES


def _ext_rows(prev_ref, tile, next_ref):
    return jnp.concatenate([prev_ref[...].astype(F32), tile.astype(F32), next_ref[...].astype(F32)], axis=1)


def _tail_kernel(xc_ref, xl_ref, xp_ref, xn_ref, mod_ref,
                 na_ref, nap_ref, nan_ref, wa_ref, wap_ref, wan_ref, of_ref, ofp_ref, ofn_ref,
                 ob_ref, obp_ref, obn_ref, go_ref, gop_ref, gon_ref,
                 gn_ref, wo_ref, g2_ref, wv_ref, wg_ref, cw_ref, cb_ref, wd_ref, fg_ref,
                 o_ref, gs_ref, *, t0, two_source, final):
    ti = pl.program_id(1) + t0
    ext = TILE + 2 * HALO
    mod = mod_ref[...]
    x_t = _pick_tile(xc_ref, xl_ref) if two_source else xl_ref[...]
    x_ext = _ext_rows(xp_ref, x_t, xn_ref)

    og = _rows2d(_ext_rows(ofp_ref, of_ref[...], ofn_ref) + _ext_rows(obp_ref, ob_ref[...], obn_ref))
    gn = gn_ref[...]
    parts = []
    for h in range(GLA_HEADS):
        seg = og[:, h * GLA_DV:(h + 1) * GLA_DV]
        ms = jnp.mean(seg * seg, axis=-1, keepdims=True)
        parts.append(seg * lax.rsqrt(ms + EPS) * gn)
    go = _rows2d(_ext_rows(gop_ref, go_ref[...], gon_ref))
    y_g = (jnp.concatenate(parts, axis=1) * (go * jax.nn.sigmoid(go))).astype(BF16)
    y_na = _rows2d(_ext_rows(nap_ref, na_ref[...], nan_ref)).astype(BF16)
    y_wa = _rows2d(_ext_rows(wap_ref, wa_ref[...], wan_ref)).astype(BF16)
    res = (jnp.dot(y_na, wo_ref[0:256, :], preferred_element_type=F32)
           + jnp.dot(y_wa, wo_ref[256:512, :], preferred_element_type=F32)
           + jnp.dot(y_g, wo_ref[512:1024, :], preferred_element_type=F32))
    x1_ext = x_ext + mod[:, :, 2 * D_MODEL:3 * D_MODEL] * res.reshape(BP, ext, D_MODEL)

    shift, scale = mod[:, :, 3 * D_MODEL:4 * D_MODEL], mod[:, :, 4 * D_MODEL:5 * D_MODEL]
    h_all = _mod_norm(x1_ext, g2_ref[...], shift, scale)
    h_ext = _rows2d(h_all).astype(BF16)
    h_b = _rows2d(h_all[:, HALO:HALO + TILE]).astype(BF16)
    seq_start = ti <= 1
    seq_end = (ti == 0) | (ti == N_TILES - 1)
    acc = jnp.zeros((BP * TILE, D_MODEL), F32)
    for c in range(FF_DIM // FF_CHUNK):
        cols = slice(c * FF_CHUNK, (c + 1) * FF_CHUNK)
        g_ext = jnp.dot(h_ext, wg_ref[:, cols], preferred_element_type=F32).reshape(BP, ext, FF_CHUNK)
        gs_ref[...] = g_ext
        gs_ref[:, 0:HALO, :] = jnp.where(seq_start, 0.0, g_ext[:, 0:HALO])
        gs_ref[:, HALO + TILE:, :] = jnp.where(seq_end, 0.0, g_ext[:, HALO + TILE:])
        cw = cw_ref[:, cols]
        gate = (cw[0:1] * gs_ref[:, pl.ds(HALO - 1, TILE), :] + cw[1:2] * gs_ref[:, pl.ds(HALO, TILE), :]
                + cw[2:3] * gs_ref[:, pl.ds(HALO + 1, TILE), :] + cb_ref[:, cols])
        gate = _rows2d(gate)
        val = jnp.dot(h_b, wv_ref[:, cols], preferred_element_type=F32)
        act = 0.5 * gate * (1.0 + lax.erf(gate * (1.0 / math.sqrt(2.0))))
        acc = acc + jnp.dot((act * val).astype(BF16), wd_ref[cols, :], preferred_element_type=F32)
    y = x1_ext[:, HALO:HALO + TILE] + mod[:, :, 5 * D_MODEL:6 * D_MODEL] * acc.reshape(BP, TILE, D_MODEL)
    if final:
        ms = jnp.mean(y * y, axis=-1, keepdims=True)
        y = y * lax.rsqrt(ms + EPS) * fg_ref[...]
    o_ref[...] = y


def _tail_call(layer, x_first, x_rest, rest_tile0, mod_rows, o_na, o_wa, o_f, o_b, go, gn, w_out,
               g2, w_up, cw, cb, wd, fg, with_ctx, final):
    B = x_first.shape[0]
    t0 = 0 if with_ctx else 1
    n_t = N_TILES - t0
    per8 = TILE // HALO

    def tile_and_halos(w):
        last8 = TOK // HALO - 1
        return [pl.BlockSpec((BP, TILE, w), lambda b, i: (b, i + t0, 0)),
                pl.BlockSpec((BP, HALO, w), lambda b, i: (b, jnp.maximum((i + t0) * per8 - 1, 0), 0)),
                pl.BlockSpec((BP, HALO, w), lambda b, i: (b, jnp.minimum((i + t0 + 1) * per8, last8), 0))]

    if with_ctx:
        spec_c, spec_l = _two_source_specs(rest_tile0)
    else:
        spec_c = spec_l = pl.BlockSpec((BP, TILE, D_MODEL), lambda b, i: (b, i + t0, 0))
    rest_last8 = x_rest.shape[1] // HALO - 1
    r0 = rest_tile0 - 1 + t0
    x_prev = pl.BlockSpec((BP, HALO, D_MODEL), lambda b, i: (b, jnp.clip((i + r0) * per8 - 1, 0, rest_last8), 0))
    x_next = pl.BlockSpec((BP, HALO, D_MODEL), lambda b, i: (b, jnp.clip((i + r0 + 1) * per8, 0, rest_last8), 0))
    once = pl.Buffered(1)
    return pl.pallas_call(
        functools.partial(_tail_kernel, t0=t0, two_source=with_ctx, final=final),
        grid=(B // BP, n_t),
        in_specs=[
            spec_c, spec_l, x_prev, x_next, _mod_spec(t0),
            *tile_and_halos(NA_WIDTH), *tile_and_halos(WA_Q_WIDTH),
            *tile_and_halos(512), *tile_and_halos(512), *tile_and_halos(512),
            _layer_spec(layer, 1, GLA_DV),
            _layer_spec(layer, MIX_WIDTH, D_MODEL, pipeline_mode=once),
            _layer_spec(layer, 1, D_MODEL),
            pl.BlockSpec((None, D_MODEL, FF_DIM), lambda b, i: (layer, 0, 0), pipeline_mode=once),
            pl.BlockSpec((None, D_MODEL, FF_DIM), lambda b, i: (layer, 0, 1), pipeline_mode=once),
            _layer_spec(layer, 3, FF_DIM), _layer_spec(layer, 1, FF_DIM),
            _layer_spec(layer, FF_DIM, D_MODEL, pipeline_mode=once),
            pl.BlockSpec((1, D_MODEL), lambda b, i: (0, 0)),
        ],
        out_specs=pl.BlockSpec((BP, TILE, D_MODEL), lambda b, i: (b, i, 0)),
        out_shape=jax.ShapeDtypeStruct((B, n_t * TILE, D_MODEL), F32),
        scratch_shapes=[pltpu.VMEM((BP, TILE + 2 * HALO, FF_CHUNK), F32)],
        compiler_params=_cparams(("arbitrary", "arbitrary")),
        name="layer_tail",
    )(x_first, x_rest, x_rest, x_rest, mod_rows,
      o_na, o_na, o_na, o_wa, o_wa, o_wa, o_f, o_f, o_f, o_b, o_b, o_b, go, go, go,
      gn, w_out, g2, w_up, w_up, cw, cb, wd, fg)


def _rope_tables():
    t = np.arange(SEQ)
    n_freq = WA_HEAD_DIM // 4
    inv = ROPE_THETA ** (-np.arange(n_freq, dtype=np.float32) / n_freq)
    ang = np.concatenate([(t // GRID_W).astype(np.float32)[:, None] * inv,
                          (t % GRID_W).astype(np.float32)[:, None] * inv], axis=-1)
    cos = np.tile(np.cos(ang), (1, 2 * WA_HEADS))
    sin = np.tile(np.sin(ang), (1, 2 * WA_HEADS))
    cos = np.concatenate([np.ones((CTX_LEN, 256), np.float32), cos], axis=0)
    sin = np.concatenate([np.zeros((CTX_LEN, 256), np.float32), sin], axis=0)
    return jnp.asarray(cos, F32), jnp.asarray(sin, F32)


def _stacked_weights(w_in, gate_w, gate_b, w_out):
    order = np.array(WA_HEAD_ORDER)
    waq = w_in[:, :, 3 * NA_WIDTH:3 * NA_WIDTH + WA_Q_WIDTH].reshape(DEPTH, D_MODEL, WA_HEADS, WA_HEAD_DIM)
    waq = waq[:, :, order].reshape(DEPTH, D_MODEL, WA_Q_WIDTH)
    lr_pad = jnp.zeros((DEPTH, D_MODEL, X_COLS - X_LR - 2 * GLA_GATE_RANK), w_in.dtype)
    w_extra = jnp.concatenate([waq, w_in[:, :, W_LR:], lr_pad], axis=-1).astype(BF16)
    zero = jnp.zeros((DEPTH, GLA_GATE_RANK, GLA_QK_WIDTH), F32)
    gw = jnp.concatenate([
        jnp.concatenate([gate_w[:, 0], zero], axis=-1),
        jnp.concatenate([zero, gate_w[:, 1]], axis=-1),
        jnp.zeros((DEPTH, LANES - 2 * GLA_GATE_RANK, 2 * GLA_QK_WIDTH), F32)], axis=1).astype(BF16)
    gb = gate_b.reshape(DEPTH, 1, 2 * GLA_QK_WIDTH)
    wo_wa = w_out[:, NA_WIDTH:NA_WIDTH + WA_Q_WIDTH].reshape(DEPTH, WA_HEADS, WA_HEAD_DIM, D_MODEL)[:, order]
    wo = jnp.concatenate([w_out[:, :NA_WIDTH], wo_wa.reshape(DEPTH, WA_Q_WIDTH, D_MODEL),
                          w_out[:, NA_WIDTH + WA_Q_WIDTH:]], axis=1).astype(BF16)
    return w_in.astype(BF16), w_extra, gw, gb, wo


def kernel(x, c, ctx, c_ctx, w_mod, b_mod, norm1_g, norm2_g, w_in, na_rpb, wa_sink, gla_gate_w,
           gla_gate_b, gla_norm_g, w_out, ffn_w_up, ffn_conv_w, ffn_conv_b, ffn_w_down, final_norm_g):
    B = x.shape[0]
    assert x.shape == (B, SEQ, D_MODEL) and ctx.shape == (B, CTX_LEN, D_MODEL) and B % BP == 0
    n_c = -(-(B + 1) // SUBLANES) * SUBLANES
    cc = jnp.concatenate([c, c_ctx[None], jnp.zeros((n_c - B - 1, D_MODEL), F32)], axis=0)
    mods = _mod_call(cc, w_mod, b_mod)
    cos_t, sin_t = _rope_tables()
    emat, lmask = _gla_constants()
    emat = jnp.asarray(emat, BF16)
    lmask = jnp.asarray(lmask, F32)
    fg = final_norm_g.reshape(1, D_MODEL)
    w_main, w_extra, gw, gb, wo = _stacked_weights(w_in, gla_gate_w, gla_gate_b, w_out)
    w_up = ffn_w_up.astype(BF16)
    w_down = ffn_w_down.astype(BF16)
    g1 = norm1_g.reshape(DEPTH, 1, D_MODEL)
    g2 = norm2_g.reshape(DEPTH, 1, D_MODEL)
    gn = gla_norm_g.reshape(DEPTH, 1, GLA_DV)
    conv_b = ffn_conv_b.reshape(DEPTH, 1, FF_DIM)

    x_first, x_rest, rest_tile0 = ctx, x, 0
    out = None
    for l in range(DEPTH):
        last = l == DEPTH - 1
        mod_rows = jnp.stack([jnp.broadcast_to(mods[l, B], (B, 6 * D_MODEL)), mods[l, :B]],
                             axis=0).reshape(2, B, 1, 6 * D_MODEL)
        na, wa, gqk, gv, go, la = _inproj_call(l, x_first, x_rest, rest_tile0, mod_rows,
                                               g1, w_main, w_extra, gw, gb, cos_t, sin_t)
        bias = _nabias_call(na_rpb[l])
        o_na, o_wa, o_f, o_b = _mixer_call(na, wa, bias, wa_sink[l], gqk, la, gv, emat, lmask, with_ctx=not last)
        x2 = _tail_call(l, x_first, x_rest, rest_tile0, mod_rows, o_na, o_wa, o_f, o_b, go,
                        gn, wo, g2, w_up, ffn_conv_w, conv_b, w_down, fg, with_ctx=not last, final=last)
        if last:
            out = x2
        else:
            x_first, x_rest, rest_tile0 = x2, x2, 1
    return out
```

```python
import functools
import math

import numpy as np
import jax
import jax.numpy as jnp
from jax import lax
from jax.experimental import pallas as pl
from jax.experimental.pallas import tpu as pltpu

D_MODEL = 1024
SEQ = 2048
DEPTH = 2
CTX_LEN = 256
GRID_W = 64
GRID_ROWS = SEQ // GRID_W
TOK = CTX_LEN + SEQ

NA_HEADS = 4
NA_HEAD_DIM = 64
NA_WIN_R = 8
NA_WIN_C = 16
WA_HEADS = 4
WA_KV_HEADS = 2
WA_HEAD_DIM = 64
WA_WINDOW = 128
WA_BLOCK = 128
GLA_HEADS = 4
GLA_DK = 64
GLA_DV = 128
GLA_GATE_RANK = 16
GLA_GATE_TAU = 16.0
GLA_CHUNK = 64

NA_WIDTH = NA_HEADS * NA_HEAD_DIM
WA_Q_WIDTH = WA_HEADS * WA_HEAD_DIM
WA_KV_WIDTH = WA_KV_HEADS * WA_HEAD_DIM
GLA_QK_WIDTH = GLA_HEADS * GLA_DK
GLA_V_WIDTH = GLA_HEADS * GLA_DV
MIX_WIDTH = NA_WIDTH + WA_Q_WIDTH + GLA_V_WIDTH
IN_WIDTH = 3 * NA_WIDTH + WA_Q_WIDTH + 2 * WA_KV_WIDTH + 2 * GLA_QK_WIDTH + 2 * GLA_V_WIDTH + 2 * GLA_GATE_RANK

FF_DIM = 2816
FF_CHUNK = FF_DIM
ROPE_THETA = 10000.0
EPS = 1e-6

BP = 2
TILE = 256
N_TILES = TOK // TILE
LANES = 128
SUBLANES = 8
NEG = -1e30
VMEM_LIMIT = 56 * 1024 * 1024

F32 = jnp.float32
BF16 = jnp.bfloat16
NT_DIMS = (((1,), (1,)), ((), ()))
TN_DIMS = (((0,), (0,)), ((), ()))

W_NA = 0
W_WAK = 1024
W_WAV = 1152
W_GQ = 1280
W_GK = 1536
W_GV = 1792
W_GO = 2304
W_LR = 2816
X_WAQ = 0
X_LR = 256
X_COLS = 384

WA_HEAD_ORDER = (0, 2, 1, 3)
Q_SCALE = NA_HEAD_DIM ** -0.5


def _cparams(sem):
    return pltpu.CompilerParams(dimension_semantics=sem, vmem_limit_bytes=VMEM_LIMIT)


def _lane_lo(shape, width=LANES):
    lane = lax.broadcasted_iota(jnp.int32, shape, len(shape) - 1)
    return (lane & (width - 1)) < (width // 2)


def _mod_kernel(c_ref, w_ref, b_ref, o_ref):
    c = c_ref[...]
    sc = c * jax.nn.sigmoid(c)
    o_ref[...] = jnp.dot(sc.astype(BF16), w_ref[...].astype(BF16),
                         preferred_element_type=F32) + b_ref[...]


def _mod_call(cc, w_mod, b_mod):
    tn = 1536
    rows = cc.shape[0]
    return pl.pallas_call(
        _mod_kernel,
        grid=(DEPTH, 6 * D_MODEL // tn),
        in_specs=[
            pl.BlockSpec((rows, D_MODEL), lambda l, j: (0, 0)),
            pl.BlockSpec((None, D_MODEL, tn), lambda l, j: (l, 0, j)),
            pl.BlockSpec((None, 1, tn), lambda l, j: (l, 0, j)),
        ],
        out_specs=pl.BlockSpec((None, rows, tn), lambda l, j: (l, 0, j)),
        out_shape=jax.ShapeDtypeStruct((DEPTH, rows, 6 * D_MODEL), F32),
        compiler_params=_cparams(("arbitrary", "arbitrary")),
        name="adaln_mod",
    )(cc, w_mod, b_mod.reshape(DEPTH, 1, 6 * D_MODEL))


N_RO = 2 * NA_WIN_R - 1
N_CO = 2 * NA_WIN_C - 1
N_BIAS = N_RO + 1


def _nabias_kernel(rpb_ref, o_ref):
    h = pl.program_id(0)
    shape = (GRID_W, LANES)
    q = lax.broadcasted_iota(jnp.int32, shape, 0)
    kc = lax.broadcasted_iota(jnp.int32, shape, 1) & (GRID_W - 1)
    d = kc - q + (NA_WIN_C - 1)
    cs = jnp.clip(q - NA_WIN_C // 2, 0, GRID_W - NA_WIN_C)
    inwin = (kc >= cs) & (kc < cs + NA_WIN_C)
    base = h * (N_RO * N_CO)
    o_ref[0] = jnp.full(shape, NEG, F32)
    for ro in range(N_RO):
        acc = jnp.full(shape, NEG, F32)
        for dd in range(N_CO):
            acc = jnp.where(d == dd, rpb_ref[base + ro * N_CO + dd], acc)
        o_ref[1 + ro] = jnp.where(inwin, acc, NEG)


def _nabias_call(rpb):
    return pl.pallas_call(
        _nabias_kernel,
        grid=(NA_HEADS,),
        in_specs=[pl.BlockSpec(memory_space=pltpu.SMEM)],
        out_specs=pl.BlockSpec((None, N_BIAS, GRID_W, LANES), lambda h: (h, 0, 0, 0)),
        out_shape=jax.ShapeDtypeStruct((NA_HEADS, N_BIAS, GRID_W, LANES), F32),
        compiler_params=_cparams(("arbitrary",)),
        name="na_bias",
    )(rpb.reshape(-1))


def _mod_norm(x, g, shift, scale):
    ms = jnp.mean(x * x, axis=-1, keepdims=True)
    return (x * lax.rsqrt(ms + EPS) * g) * (1.0 + scale) + shift


def _pick_tile(first_ref, rest_ref):
    return jnp.where(pl.program_id(1) == 0, first_ref[...], rest_ref[...])


def _rows2d(a):
    return a.reshape(a.shape[0] * a.shape[1], a.shape[2])


def _inproj_kernel(xc_ref, xl_ref, mod_ref, g1_ref, w_ref, wx_ref, gw_ref, gb_ref, cos_ref, sin_ref,
                   na_ref, wa_ref, gqk_ref, gv_ref, go_ref, la_ref):
    mod = mod_ref[...]
    h = _mod_norm(_pick_tile(xc_ref, xl_ref), g1_ref[...], mod[:, :, 0:D_MODEL], mod[:, :, D_MODEL:2 * D_MODEL])
    hb = _rows2d(h).astype(BF16)

    p_main = jnp.dot(hb, w_ref[...], preferred_element_type=F32)
    p_extra = jnp.dot(hb, wx_ref[...], preferred_element_type=F32)

    def main(c0, c1):
        return p_main[:, c0:c1].reshape(BP, TILE, c1 - c0)

    def extra(c0, c1):
        return p_extra[:, c0:c1].reshape(BP, TILE, c1 - c0)

    na_ref[:, :, 0:NA_WIDTH] = (main(W_NA, W_NA + NA_WIDTH) * Q_SCALE).astype(BF16)
    na_ref[:, :, NA_WIDTH:3 * NA_WIDTH] = main(W_NA + NA_WIDTH, W_NA + 3 * NA_WIDTH).astype(BF16)
    cos = cos_ref[...]
    sin = sin_ref[...]

    def rotate_half(a):
        groups = []
        for g in range(a.shape[-1] // LANES):
            blk = _rows2d(a[:, :, g * LANES:(g + 1) * LANES])
            lane = lax.broadcasted_iota(jnp.int32, blk.shape, 1)
            first_half = (lane & (WA_HEAD_DIM // 2)) == 0
            rot = jnp.where(first_half, -pltpu.roll(blk, LANES - WA_HEAD_DIM // 2, 1),
                            pltpu.roll(blk, WA_HEAD_DIM // 2, 1))
            groups.append(rot.reshape(BP, TILE, LANES))
        return jnp.concatenate(groups, axis=-1)

    wa_q = extra(X_WAQ, X_LR)
    wa_k = main(W_WAK, W_WAV)
    wa_ref[:, :, 0:256] = ((wa_q * cos + rotate_half(wa_q) * sin) * Q_SCALE).astype(BF16)
    wa_ref[:, :, 256:384] = (wa_k * cos[:, 0:LANES] + rotate_half(wa_k) * sin[:, 0:LANES]).astype(BF16)
    wa_ref[:, :, 384:512] = main(W_WAV, W_GQ).astype(BF16)
    gqk_ref[:, :, 0:GLA_QK_WIDTH] = main(W_GQ, W_GK) * Q_SCALE
    gqk_ref[:, :, GLA_QK_WIDTH:] = main(W_GK, W_GV)
    gv_ref[...] = main(W_GV, W_GO).astype(BF16)
    go_ref[...] = main(W_GO, W_LR).astype(BF16)
    lr = p_extra[:, X_LR:X_COLS].astype(BF16)
    logit = jnp.dot(lr, gw_ref[...], preferred_element_type=F32) + gb_ref[...]
    log_sig = jnp.minimum(logit, 0.0) - jnp.log(1.0 + jnp.exp(-jnp.abs(logit)))
    la_ref[...] = (log_sig * (1.0 / GLA_GATE_TAU)).reshape(BP, TILE, 2 * GLA_QK_WIDTH)


def _two_source_specs(rest_tile0):
    return (pl.BlockSpec((BP, TILE, D_MODEL), lambda b, i: (b, 0, 0)),
            pl.BlockSpec((BP, TILE, D_MODEL), lambda b, i: (b, jnp.maximum(i - 1, 0) + rest_tile0, 0)))


def _mod_spec(t0):
    return pl.BlockSpec((None, BP, 1, 6 * D_MODEL), lambda b, i: (jnp.minimum(i + t0, 1), b, 0, 0))


def _layer_spec(layer, r, c, **kw):
    return pl.BlockSpec((None, r, c), lambda b, i: (layer, 0, 0), **kw)


def _inproj_call(layer, x_first, x_rest, rest_tile0, mod_rows, g1, w_main, w_extra, gw, gb, cos_t, sin_t):
    B = x_first.shape[0]
    tile = lambda w: pl.BlockSpec((BP, TILE, w), lambda b, i: (b, i, 0))
    spec_c, spec_l = _two_source_specs(rest_tile0)
    return pl.pallas_call(
        _inproj_kernel,
        grid=(B // BP, N_TILES),
        in_specs=[
            spec_c, spec_l,
            _mod_spec(0),
            _layer_spec(layer, 1, D_MODEL),
            _layer_spec(layer, D_MODEL, IN_WIDTH),
            _layer_spec(layer, D_MODEL, X_COLS),
            _layer_spec(layer, LANES, 2 * GLA_QK_WIDTH),
            _layer_spec(layer, 1, 2 * GLA_QK_WIDTH),
            pl.BlockSpec((TILE, 256), lambda b, i: (i, 0)),
            pl.BlockSpec((TILE, 256), lambda b, i: (i, 0)),
        ],
        out_specs=[tile(768), tile(512), tile(512), tile(512), tile(512), tile(512)],
        out_shape=[
            jax.ShapeDtypeStruct((B, TOK, 768), BF16),
            jax.ShapeDtypeStruct((B, TOK, 512), BF16),
            jax.ShapeDtypeStruct((B, TOK, 512), F32),
            jax.ShapeDtypeStruct((B, TOK, 512), BF16),
            jax.ShapeDtypeStruct((B, TOK, 512), BF16),
            jax.ShapeDtypeStruct((B, TOK, 512), F32),
        ],
        compiler_params=_cparams(("arbitrary", "arbitrary")),
        name="in_proj",
    )(x_first, x_rest, mod_rows, g1, w_main, w_extra, gw, gb, cos_t, sin_t)


NA_ROWS_PER_STEP = TILE // GRID_W
NA_STEPS = GRID_ROWS // NA_ROWS_PER_STEP
NA_UNION = NA_WIN_R + NA_ROWS_PER_STEP


def _split_heads_rows(q):
    lo = _lane_lo(q.shape)
    zero = jnp.zeros_like(q)
    return jnp.concatenate([jnp.where(lo, q, zero), jnp.where(lo, zero, q)], axis=0)


def _merge_heads_rows(o, n):
    return jnp.where(_lane_lo((n, LANES)), o[0:n], o[n:2 * n])


def _na_rows(j, slab_ref, bias_ref, o_ref):
    n_union = NA_UNION * GRID_W
    u0 = jnp.clip(j * NA_ROWS_PER_STEP - NA_WIN_R // 2, 0, GRID_ROWS - NA_UNION)
    qrow = pl.multiple_of(CTX_LEN + j * TILE, TILE)
    krow = pl.multiple_of(CTX_LEN + u0 * GRID_W, GRID_W)
    lo = _lane_lo((GRID_W, LANES))
    tile_idx = []
    for a in range(NA_ROWS_PER_STEP):
        r = j * NA_ROWS_PER_STEP + a
        rs = jnp.clip(r - NA_WIN_R // 2, 0, GRID_ROWS - NA_WIN_R)
        idx = []
        for u in range(NA_UNION):
            key_row = u0 + u
            inside = (key_row >= rs) & (key_row < rs + NA_WIN_R)
            idx.append(jnp.where(inside, key_row - r + NA_WIN_R, 0))
        tile_idx.append(idx)
    for p in range(NA_HEADS // 2):
        kc = slab_ref[0:CTX_LEN, 256 + p * LANES:256 + (p + 1) * LANES]
        vc = slab_ref[0:CTX_LEN, 512 + p * LANES:512 + (p + 1) * LANES]
        ku = slab_ref[pl.ds(krow, n_union), 256 + p * LANES:256 + (p + 1) * LANES]
        vu = slab_ref[pl.ds(krow, n_union), 512 + p * LANES:512 + (p + 1) * LANES]
        q_all = slab_ref[pl.ds(qrow, TILE), p * LANES:(p + 1) * LANES]
        q2 = jnp.concatenate([_split_heads_rows(q_all[a * GRID_W:(a + 1) * GRID_W])
                              for a in range(NA_ROWS_PER_STEP)], axis=0)
        bias = jnp.concatenate([
            jnp.concatenate([jnp.where(lo, bias_ref[2 * p + hh, tile_idx[a][2 * jj]],
                                       bias_ref[2 * p + hh, tile_idx[a][2 * jj + 1]])
                             for jj in range(NA_UNION // 2)], axis=1)
            for a in range(NA_ROWS_PER_STEP) for hh in range(2)], axis=0)
        s_u = lax.dot_general(q2, ku, NT_DIMS, preferred_element_type=F32) + bias
        s_c = lax.dot_general(q2, kc, NT_DIMS, preferred_element_type=F32)
        m = jnp.maximum(jnp.max(s_u, axis=-1, keepdims=True), jnp.max(s_c, axis=-1, keepdims=True))
        p_u = jnp.exp(s_u - m)
        p_c = jnp.exp(s_c - m)
        den = jnp.sum(p_u, axis=-1, keepdims=True) + jnp.sum(p_c, axis=-1, keepdims=True)
        o = (jnp.dot(p_u.astype(BF16), vu, preferred_element_type=F32)
             + jnp.dot(p_c.astype(BF16), vc, preferred_element_type=F32)) / den
        out = [_merge_heads_rows(o[a * 2 * GRID_W:(a + 1) * 2 * GRID_W], GRID_W)
               for a in range(NA_ROWS_PER_STEP)]
        o_ref[:, p * LANES:(p + 1) * LANES] = jnp.concatenate(out, axis=0).astype(BF16)


def _na_ctx(slab_ref, o_ref):
    for p in range(NA_HEADS // 2):
        kc = slab_ref[0:CTX_LEN, 256 + p * LANES:256 + (p + 1) * LANES]
        vc = slab_ref[0:CTX_LEN, 512 + p * LANES:512 + (p + 1) * LANES]
        q2 = _split_heads_rows(slab_ref[0:CTX_LEN, p * LANES:(p + 1) * LANES])
        s_c = lax.dot_general(q2, kc, NT_DIMS, preferred_element_type=F32)
        p_c = jnp.exp(s_c - jnp.max(s_c, axis=-1, keepdims=True))
        den = jnp.sum(p_c, axis=-1, keepdims=True)
        o = jnp.dot(p_c.astype(BF16), vc, preferred_element_type=F32) / den
        o_ref[:, p * LANES:(p + 1) * LANES] = _merge_heads_rows(o, CTX_LEN).astype(BF16)


N_WA_BLOCKS = SEQ // WA_BLOCK
WA_PER_STEP = TILE // WA_BLOCK
WA_KCOL = slice(256, 384)
WA_VCOL = slice(384, 512)


def _wa_stack_heads(slab_ref, qrow):
    lo = _lane_lo((WA_BLOCK, LANES))
    qa = slab_ref[pl.ds(qrow, WA_BLOCK), 0:LANES]
    qb = slab_ref[pl.ds(qrow, WA_BLOCK), LANES:2 * LANES]
    zero = jnp.zeros_like(qa)
    return jnp.concatenate([jnp.where(lo, qa, zero), jnp.where(lo, zero, qa),
                            jnp.where(lo, qb, zero), jnp.where(lo, zero, qb)], axis=0)


def _wa_softmax_pv(sink_ref, scores, values, o_ref, out_row):
    rows4 = WA_HEADS * WA_BLOCK
    slot = lax.broadcasted_iota(jnp.int32, (rows4, 1), 0) // WA_BLOCK
    sink = jnp.zeros((rows4, 1), F32)
    for s_i, h in enumerate(WA_HEAD_ORDER):
        sink = jnp.where(slot == s_i, sink_ref[h], sink)
    m = sink
    for s in scores:
        m = jnp.maximum(m, jnp.max(s, axis=-1, keepdims=True))
    den = jnp.exp(sink - m)
    o = None
    for s, v in zip(scores, values):
        e = jnp.exp(s - m)
        den = den + jnp.sum(e, axis=-1, keepdims=True)
        t = jnp.dot(e.astype(BF16), v, preferred_element_type=F32)
        o = t if o is None else o + t
    o = o / den
    b = WA_BLOCK
    lo = _lane_lo((b, LANES))
    rows = slice(out_row, out_row + b)
    o_ref[rows, 0:LANES] = jnp.where(lo, o[0:b], o[b:2 * b]).astype(BF16)
    o_ref[rows, LANES:2 * LANES] = jnp.where(lo, o[2 * b:3 * b], o[3 * b:4 * b]).astype(BF16)


def _wa_scores(q4, k):
    return lax.dot_general(q4, k, NT_DIMS, preferred_element_type=F32)


def _wa_block(n, out_row, sink_ref, slab_ref, o_ref):
    rows4 = WA_HEADS * WA_BLOCK
    qrow = pl.multiple_of(CTX_LEN + n * WA_BLOCK, WA_BLOCK)
    prow = pl.multiple_of(qrow - WA_BLOCK, WA_BLOCK)
    nrow = pl.multiple_of(jnp.minimum(qrow + WA_BLOCK, TOK - WA_BLOCK), WA_BLOCK)
    k_blocks = [slab_ref[pl.ds(s, WA_BLOCK), WA_KCOL] for s in (prow, qrow, nrow)]
    v_blocks = [slab_ref[pl.ds(s, WA_BLOCK), WA_VCOL] for s in (prow, qrow, nrow)]
    i = lax.broadcasted_iota(jnp.int32, (rows4, WA_BLOCK), 0) & (WA_BLOCK - 1)
    jj = lax.broadcasted_iota(jnp.int32, (rows4, WA_BLOCK), 1)
    keep_prev = (jj >= i) & (n > 0)
    keep_next = (jj <= i) & (n < N_WA_BLOCKS - 1)
    q4 = _wa_stack_heads(slab_ref, qrow)
    s_p = jnp.where(keep_prev, _wa_scores(q4, k_blocks[0]), NEG)
    s_m = _wa_scores(q4, k_blocks[1])
    s_n = jnp.where(keep_next, _wa_scores(q4, k_blocks[2]), NEG)
    s_c = _wa_scores(q4, slab_ref[0:CTX_LEN, WA_KCOL])
    _wa_softmax_pv(sink_ref, [s_p, s_m, s_n, s_c], v_blocks + [slab_ref[0:CTX_LEN, WA_VCOL]], o_ref, out_row)


def _wa_ctx(blk, sink_ref, slab_ref, o_ref):
    q4 = _wa_stack_heads(slab_ref, blk * WA_BLOCK)
    s_c = _wa_scores(q4, slab_ref[0:CTX_LEN, WA_KCOL])
    _wa_softmax_pv(sink_ref, [s_c], [slab_ref[0:CTX_LEN, WA_VCOL]], o_ref, blk * WA_BLOCK)


GLA_LEVELS = int(math.log2(GLA_CHUNK))
GLA_ROW_LEVELS = 3
E_ROWS = (1 + GLA_LEVELS - GLA_ROW_LEVELS) * GLA_CHUNK


def _gla_constants():
    C = GLA_CHUNK
    p = np.arange(C)
    r = p[None, :]
    mats = [r <= p[:, None]]
    masks = []
    for lvl in range(GLA_LEVELS):
        m = C >> (lvl + 1)
        pair = p // (2 * m)
        half = (p // m) % 2
        a_end = pair * 2 * m + m - 1
        if lvl >= GLA_ROW_LEVELS:
            mats.append(r <= a_end[:, None])
        masks.append((pair[:, None] == pair[None, :]) & (half[:, None] == 1) & (half[None, :] == 0))
    masks.append(p[:, None] == p[None, :])
    fwd = np.concatenate(mats, axis=0).astype(np.float32)
    bwd = np.concatenate([blk[::-1, ::-1] for blk in mats], axis=0).astype(np.float32)
    msk = np.stack(masks).astype(np.float32)
    lmask = np.stack([msk, msk[:, ::-1, ::-1]])
    lmask = np.concatenate([lmask, lmask], axis=-1)
    return np.stack([fwd, bwd]), lmask


def _gla_tiles(emat_ref, lmask_ref, qkf_ref, laf_ref, vf_ref, qkb_ref, lab_ref, vb_ref,
               of_ref, ob_ref, st_ref):
    C = GLA_CHUNK
    lo = _lane_lo((C, LANES))
    v_lo = _lane_lo((C, 2 * GLA_DV), width=2 * GLA_DV)
    st_rows = lax.broadcasted_iota(jnp.int32, (2 * GLA_DV, LANES), 0) < GLA_DV
    st_keep = st_rows == _lane_lo((2 * GLA_DV, LANES))

    def head_block(kk):
        zero = jnp.zeros_like(kk)
        return jnp.concatenate([jnp.where(lo, kk, zero), jnp.where(lo, zero, kk)], axis=0)

    def chunk(bb, d, p, r0, qk_ref, la_ref, v_ref, o_ref):
        rows = slice(r0, r0 + C)
        g = la_ref[bb, rows, p * LANES:(p + 1) * LANES]
        q = qk_ref[bb, rows, p * LANES:(p + 1) * LANES]
        k = qk_ref[bb, rows, GLA_QK_WIDTH + p * LANES:GLA_QK_WIDTH + (p + 1) * LANES]
        v = v_ref[bb, rows, p * 2 * GLA_DV:(p + 1) * 2 * GLA_DV]
        g_hi = g.astype(BF16)
        g_lo = (g - g_hi.astype(F32)).astype(BF16)
        e2 = jnp.dot(emat_ref[d], jnp.concatenate([g_hi, g_lo], axis=1), preferred_element_type=F32)
        e = e2[:, 0:LANES] + e2[:, LANES:2 * LANES]
        b = e[0:C]
        end_row = C - 1 if d == 0 else 0
        b_end = b[end_row:end_row + 1]
        st = st_ref[bb, d, p]
        o = lax.dot_general((q * jnp.exp(b)).astype(BF16), st.astype(BF16), NT_DIMS,
                            preferred_element_type=F32)
        att = lax.dot_general(q.astype(BF16), head_block(k.astype(BF16)), NT_DIMS,
                              preferred_element_type=F32) * lmask_ref[d, GLA_LEVELS]
        for lvl in range(GLA_LEVELS):
            m = C >> (lvl + 1)
            if lvl < GLA_ROW_LEVELS:
                pieces = []
                for u in range(C // (2 * m)):
                    row = 2 * m * u + (m - 1 if d == 0 else m)
                    pieces.append(jnp.broadcast_to(b[row:row + 1], (2 * m, LANES)))
                rho = pieces[0] if len(pieces) == 1 else jnp.concatenate(pieces, axis=0)
            else:
                rho = e[(lvl - GLA_ROW_LEVELS + 1) * C:(lvl - GLA_ROW_LEVELS + 2) * C]
            x = jnp.exp(-jnp.abs(b - rho))
            att = att + lax.dot_general((q * x).astype(BF16), head_block((k * x).astype(BF16)), NT_DIMS,
                                        preferred_element_type=F32) * lmask_ref[d, lvl]
        vzero = jnp.zeros_like(v)
        v_bd = jnp.concatenate([jnp.where(v_lo, v, vzero), jnp.where(v_lo, vzero, v)], axis=0)
        o = o + jnp.dot(att.astype(BF16), v_bd, preferred_element_type=F32)
        o_ref[bb, rows, p * 2 * GLA_DV:(p + 1) * 2 * GLA_DV] = o.astype(BF16)
        ks = (k * jnp.exp(b_end - b)).astype(BF16)
        upd = lax.dot_general(v, ks, TN_DIMS, preferred_element_type=F32)
        st_ref[bb, d, p] = st * jnp.exp(b_end) + jnp.where(st_keep, upd, 0.0)

    n_chunks = TILE // C
    for step in range(n_chunks):
        for bb in range(BP):
            for p in range(GLA_HEADS // 2):
                chunk(bb, 0, p, step * C, qkf_ref, laf_ref, vf_ref, of_ref)
                chunk(bb, 1, p, (n_chunks - 1 - step) * C, qkb_ref, lab_ref, vb_ref, ob_ref)


def _mixer_kernel(sink_ref, na_ref, wa_ref, bias_ref, emat_ref, lmask_ref, qkf_ref, laf_ref, vf_ref,
                  qkb_ref, lab_ref, vb_ref, ona_ref, owa_ref, of_ref, ob_ref, st_ref, *, with_ctx):
    i = pl.program_id(1)
    gla = functools.partial(_gla_tiles, emat_ref, lmask_ref, qkf_ref, laf_ref, vf_ref, qkb_ref, lab_ref, vb_ref,
                            of_ref, ob_ref, st_ref)

    @pl.when(i == 0)
    def _():
        st_ref[...] = jnp.zeros_like(st_ref)
        gla()
        for bb in range(BP):
            if with_ctx:
                _na_ctx(na_ref.at[bb], ona_ref.at[bb])
                for blk in range(CTX_LEN // WA_BLOCK):
                    _wa_ctx(blk, sink_ref, wa_ref.at[bb], owa_ref.at[bb])
            else:
                ona_ref[bb] = jnp.zeros((TILE, NA_WIDTH), BF16)
                owa_ref[bb] = jnp.zeros((TILE, WA_Q_WIDTH), BF16)

    @pl.when(i > 0)
    def _():
        gla()
        for bb in range(BP):
            _na_rows(i - 1, na_ref.at[bb], bias_ref, ona_ref.at[bb])
            for qb in range(WA_PER_STEP):
                _wa_block((i - 1) * WA_PER_STEP + qb, qb * WA_BLOCK, sink_ref, wa_ref.at[bb], owa_ref.at[bb])


def _mixer_call(na, wa, bias, sink, gqk, la, gv, emat, lmask, with_ctx):
    B = gqk.shape[0]
    fwd = lambda b, i: (b, i, 0)
    bwd = lambda b, i: (b, jnp.where(i == 0, 0, N_TILES - i), 0)
    bwd_la = lambda b, i: (b, jnp.where(i == 0, 0, N_TILES - i), 1)
    slab = lambda w: pl.BlockSpec((BP, TOK, w), lambda b, i: (b, 0, 0))
    return pl.pallas_call(
        functools.partial(_mixer_kernel, with_ctx=with_ctx),
        grid=(B // BP, N_TILES),
        in_specs=[
            pl.BlockSpec(memory_space=pltpu.SMEM),
            slab(768), slab(512),
            pl.BlockSpec((NA_HEADS, N_BIAS, GRID_W, LANES), lambda b, i: (0, 0, 0, 0)),
            pl.BlockSpec((2, E_ROWS, GLA_CHUNK), lambda b, i: (0, 0, 0)),
            pl.BlockSpec((2, GLA_LEVELS + 1, GLA_CHUNK, LANES), lambda b, i: (0, 0, 0, 0)),
            pl.BlockSpec((BP, TILE, 512), fwd),
            pl.BlockSpec((BP, TILE, 256), fwd),
            pl.BlockSpec((BP, TILE, 512), fwd),
            pl.BlockSpec((BP, TILE, 512), bwd),
            pl.BlockSpec((BP, TILE, 256), bwd_la),
            pl.BlockSpec((BP, TILE, 512), bwd),
        ],
        out_specs=[pl.BlockSpec((BP, TILE, NA_WIDTH), fwd), pl.BlockSpec((BP, TILE, WA_Q_WIDTH), fwd),
                   pl.BlockSpec((BP, TILE, 512), fwd), pl.BlockSpec((BP, TILE, 512), bwd)],
        out_shape=[jax.ShapeDtypeStruct((B, TOK, NA_WIDTH), BF16), jax.ShapeDtypeStruct((B, TOK, WA_Q_WIDTH), BF16),
                   jax.ShapeDtypeStruct((B, TOK, 512), BF16), jax.ShapeDtypeStruct((B, TOK, 512), BF16)],
        scratch_shapes=[pltpu.VMEM((BP, 2, GLA_HEADS // 2, 2 * GLA_DV, LANES), F32)],
        compiler_params=_cparams(("arbitrary", "arbitrary")),
        name="token_mixers",
    )(sink, na, wa, bias, emat, lmask, gqk, la, gv, gqk, la, gv)


HALO = SUBLANES


def _ext_rows(prev_ref, tile, next_ref):
    return jnp.concatenate([prev_ref[...].astype(F32), tile.astype(F32), next_ref[...].astype(F32)], axis=1)


def _tail_kernel(xc_ref, xl_ref, xp_ref, xn_ref, mod_ref,
                 na_ref, nap_ref, nan_ref, wa_ref, wap_ref, wan_ref, of_ref, ofp_ref, ofn_ref,
                 ob_ref, obp_ref, obn_ref, go_ref, gop_ref, gon_ref,
                 gn_ref, wo_ref, g2_ref, wv_ref, wg_ref, cw_ref, cb_ref, wd_ref, fg_ref,
                 o_ref, gs_ref, *, t0, two_source, final):
    ti = pl.program_id(1) + t0
    ext = TILE + 2 * HALO
    mod = mod_ref[...]
    x_t = _pick_tile(xc_ref, xl_ref) if two_source else xl_ref[...]
    x_ext = _ext_rows(xp_ref, x_t, xn_ref)

    og = _rows2d(_ext_rows(ofp_ref, of_ref[...], ofn_ref) + _ext_rows(obp_ref, ob_ref[...], obn_ref))
    gn = gn_ref[...]
    parts = []
    for h in range(GLA_HEADS):
        seg = og[:, h * GLA_DV:(h + 1) * GLA_DV]
        ms = jnp.mean(seg * seg, axis=-1, keepdims=True)
        parts.append(seg * lax.rsqrt(ms + EPS) * gn)
    go = _rows2d(_ext_rows(gop_ref, go_ref[...], gon_ref))
    y_g = (jnp.concatenate(parts, axis=1) * (go * jax.nn.sigmoid(go))).astype(BF16)
    y_na = _rows2d(_ext_rows(nap_ref, na_ref[...], nan_ref)).astype(BF16)
    y_wa = _rows2d(_ext_rows(wap_ref, wa_ref[...], wan_ref)).astype(BF16)
    res = (jnp.dot(y_na, wo_ref[0:256, :], preferred_element_type=F32)
           + jnp.dot(y_wa, wo_ref[256:512, :], preferred_element_type=F32)
           + jnp.dot(y_g, wo_ref[512:1024, :], preferred_element_type=F32))
    x1_ext = x_ext + mod[:, :, 2 * D_MODEL:3 * D_MODEL] * res.reshape(BP, ext, D_MODEL)

    shift, scale = mod[:, :, 3 * D_MODEL:4 * D_MODEL], mod[:, :, 4 * D_MODEL:5 * D_MODEL]
    h_all = _mod_norm(x1_ext, g2_ref[...], shift, scale)
    h_ext = _rows2d(h_all).astype(BF16)
    h_b = _rows2d(h_all[:, HALO:HALO + TILE]).astype(BF16)
    seq_start = ti <= 1
    seq_end = (ti == 0) | (ti == N_TILES - 1)
    acc = jnp.zeros((BP * TILE, D_MODEL), F32)
    for c in range(FF_DIM // FF_CHUNK):
        cols = slice(c * FF_CHUNK, (c + 1) * FF_CHUNK)
        g_ext = jnp.dot(h_ext, wg_ref[:, cols], preferred_element_type=F32).reshape(BP, ext, FF_CHUNK)
        gs_ref[...] = g_ext
        gs_ref[:, 0:HALO, :] = jnp.where(seq_start, 0.0, g_ext[:, 0:HALO])
        gs_ref[:, HALO + TILE:, :] = jnp.where(seq_end, 0.0, g_ext[:, HALO + TILE:])
        cw = cw_ref[:, cols]
        gate = (cw[0:1] * gs_ref[:, pl.ds(HALO - 1, TILE), :] + cw[1:2] * gs_ref[:, pl.ds(HALO, TILE), :]
                + cw[2:3] * gs_ref[:, pl.ds(HALO + 1, TILE), :] + cb_ref[:, cols])
        gate = _rows2d(gate)
        val = jnp.dot(h_b, wv_ref[:, cols], preferred_element_type=F32)
        act = 0.5 * gate * (1.0 + lax.erf(gate * (1.0 / math.sqrt(2.0))))
        acc = acc + jnp.dot((act * val).astype(BF16), wd_ref[cols, :], preferred_element_type=F32)
    y = x1_ext[:, HALO:HALO + TILE] + mod[:, :, 5 * D_MODEL:6 * D_MODEL] * acc.reshape(BP, TILE, D_MODEL)
    if final:
        ms = jnp.mean(y * y, axis=-1, keepdims=True)
        y = y * lax.rsqrt(ms + EPS) * fg_ref[...]
    o_ref[...] = y


def _tail_call(layer, x_first, x_rest, rest_tile0, mod_rows, o_na, o_wa, o_f, o_b, go, gn, w_out,
               g2, w_up, cw, cb, wd, fg, with_ctx, final):
    B = x_first.shape[0]
    t0 = 0 if with_ctx else 1
    n_t = N_TILES - t0
    per8 = TILE // HALO

    def tile_and_halos(w):
        last8 = TOK // HALO - 1
        return [pl.BlockSpec((BP, TILE, w), lambda b, i: (b, i + t0, 0)),
                pl.BlockSpec((BP, HALO, w), lambda b, i: (b, jnp.maximum((i + t0) * per8 - 1, 0), 0)),
                pl.BlockSpec((BP, HALO, w), lambda b, i: (b, jnp.minimum((i + t0 + 1) * per8, last8), 0))]

    if with_ctx:
        spec_c, spec_l = _two_source_specs(rest_tile0)
    else:
        spec_c = spec_l = pl.BlockSpec((BP, TILE, D_MODEL), lambda b, i: (b, i + t0, 0))
    rest_last8 = x_rest.shape[1] // HALO - 1
    r0 = rest_tile0 - 1 + t0
    x_prev = pl.BlockSpec((BP, HALO, D_MODEL), lambda b, i: (b, jnp.clip((i + r0) * per8 - 1, 0, rest_last8), 0))
    x_next = pl.BlockSpec((BP, HALO, D_MODEL), lambda b, i: (b, jnp.clip((i + r0 + 1) * per8, 0, rest_last8), 0))
    once = pl.Buffered(1)
    return pl.pallas_call(
        functools.partial(_tail_kernel, t0=t0, two_source=with_ctx, final=final),
        grid=(B // BP, n_t),
        in_specs=[
            spec_c, spec_l, x_prev, x_next, _mod_spec(t0),
            *tile_and_halos(NA_WIDTH), *tile_and_halos(WA_Q_WIDTH),
            *tile_and_halos(512), *tile_and_halos(512), *tile_and_halos(512),
            _layer_spec(layer, 1, GLA_DV),
            _layer_spec(layer, MIX_WIDTH, D_MODEL, pipeline_mode=once),
            _layer_spec(layer, 1, D_MODEL),
            pl.BlockSpec((None, D_MODEL, FF_DIM), lambda b, i: (layer, 0, 0), pipeline_mode=once),
            pl.BlockSpec((None, D_MODEL, FF_DIM), lambda b, i: (layer, 0, 1), pipeline_mode=once),
            _layer_spec(layer, 3, FF_DIM), _layer_spec(layer, 1, FF_DIM),
            _layer_spec(layer, FF_DIM, D_MODEL, pipeline_mode=once),
            pl.BlockSpec((1, D_MODEL), lambda b, i: (0, 0)),
        ],
        out_specs=pl.BlockSpec((BP, TILE, D_MODEL), lambda b, i: (b, i, 0)),
        out_shape=jax.ShapeDtypeStruct((B, n_t * TILE, D_MODEL), F32),
        scratch_shapes=[pltpu.VMEM((BP, TILE + 2 * HALO, FF_CHUNK), F32)],
        compiler_params=_cparams(("arbitrary", "arbitrary")),
        name="layer_tail",
    )(x_first, x_rest, x_rest, x_rest, mod_rows,
      o_na, o_na, o_na, o_wa, o_wa, o_wa, o_f, o_f, o_f, o_b, o_b, o_b, go, go, go,
      gn, w_out, g2, w_up, w_up, cw, cb, wd, fg)


def _rope_tables():
    t = np.arange(SEQ)
    n_freq = WA_HEAD_DIM // 4
    inv = ROPE_THETA ** (-np.arange(n_freq, dtype=np.float32) / n_freq)
    ang = np.concatenate([(t // GRID_W).astype(np.float32)[:, None] * inv,
                          (t % GRID_W).astype(np.float32)[:, None] * inv], axis=-1)
    cos = np.tile(np.cos(ang), (1, 2 * WA_HEADS))
    sin = np.tile(np.sin(ang), (1, 2 * WA_HEADS))
    cos = np.concatenate([np.ones((CTX_LEN, 256), np.float32), cos], axis=0)
    sin = np.concatenate([np.zeros((CTX_LEN, 256), np.float32), sin], axis=0)
    return jnp.asarray(cos, F32), jnp.asarray(sin, F32)


def _stacked_weights(w_in, gate_w, gate_b, w_out):
    order = np.array(WA_HEAD_ORDER)
    waq = w_in[:, :, 3 * NA_WIDTH:3 * NA_WIDTH + WA_Q_WIDTH].reshape(DEPTH, D_MODEL, WA_HEADS, WA_HEAD_DIM)
    waq = waq[:, :, order].reshape(DEPTH, D_MODEL, WA_Q_WIDTH)
    lr_pad = jnp.zeros((DEPTH, D_MODEL, X_COLS - X_LR - 2 * GLA_GATE_RANK), w_in.dtype)
    w_extra = jnp.concatenate([waq, w_in[:, :, W_LR:], lr_pad], axis=-1).astype(BF16)
    zero = jnp.zeros((DEPTH, GLA_GATE_RANK, GLA_QK_WIDTH), F32)
    gw = jnp.concatenate([
        jnp.concatenate([gate_w[:, 0], zero], axis=-1),
        jnp.concatenate([zero, gate_w[:, 1]], axis=-1),
        jnp.zeros((DEPTH, LANES - 2 * GLA_GATE_RANK, 2 * GLA_QK_WIDTH), F32)], axis=1).astype(BF16)
    gb = gate_b.reshape(DEPTH, 1, 2 * GLA_QK_WIDTH)
    wo_wa = w_out[:, NA_WIDTH:NA_WIDTH + WA_Q_WIDTH].reshape(DEPTH, WA_HEADS, WA_HEAD_DIM, D_MODEL)[:, order]
    wo = jnp.concatenate([w_out[:, :NA_WIDTH], wo_wa.reshape(DEPTH, WA_Q_WIDTH, D_MODEL),
                          w_out[:, NA_WIDTH + WA_Q_WIDTH:]], axis=1).astype(BF16)
    return w_in.astype(BF16), w_extra, gw, gb, wo


def kernel(x, c, ctx, c_ctx, w_mod, b_mod, norm1_g, norm2_g, w_in, na_rpb, wa_sink, gla_gate_w,
           gla_gate_b, gla_norm_g, w_out, ffn_w_up, ffn_conv_w, ffn_conv_b, ffn_w_down, final_norm_g):
    B = x.shape[0]
    assert x.shape == (B, SEQ, D_MODEL) and ctx.shape == (B, CTX_LEN, D_MODEL) and B % BP == 0
    n_c = -(-(B + 1) // SUBLANES) * SUBLANES
    cc = jnp.concatenate([c, c_ctx[None], jnp.zeros((n_c - B - 1, D_MODEL), F32)], axis=0)
    mods = _mod_call(cc, w_mod, b_mod)
    cos_t, sin_t = _rope_tables()
    emat, lmask = _gla_constants()
    emat = jnp.asarray(emat, BF16)
    lmask = jnp.asarray(lmask, F32)
    fg = final_norm_g.reshape(1, D_MODEL)
    w_main, w_extra, gw, gb, wo = _stacked_weights(w_in, gla_gate_w, gla_gate_b, w_out)
    w_up = ffn_w_up.astype(BF16)
    w_down = ffn_w_down.astype(BF16)
    g1 = norm1_g.reshape(DEPTH, 1, D_MODEL)
    g2 = norm2_g.reshape(DEPTH, 1, D_MODEL)
    gn = gla_norm_g.reshape(DEPTH, 1, GLA_DV)
    conv_b = ffn_conv_b.reshape(DEPTH, 1, FF_DIM)

    x_first, x_rest, rest_tile0 = ctx, x, 0
    out = None
    for l in range(DEPTH):
        last = l == DEPTH - 1
        mod_rows = jnp.stack([jnp.broadcast_to(mods[l, B], (B, 6 * D_MODEL)), mods[l, :B]],
                             axis=0).reshape(2, B, 1, 6 * D_MODEL)
        na, wa, gqk, gv, go, la = _inproj_call(l, x_first, x_rest, rest_tile0, mod_rows,
                                               g1, w_main, w_extra, gw, gb, cos_t, sin_t)
        bias = _nabias_call(na_rpb[l])
        o_na, o_wa, o_f, o_b = _mixer_call(na, wa, bias, wa_sink[l], gqk, la, gv, emat, lmask, with_ctx=not last)
        x2 = _tail_call(l, x_first, x_rest, rest_tile0, mod_rows, o_na, o_wa, o_f, o_b, go,
                        gn, wo, g2, w_up, ffn_conv_w, conv_b, w_down, fg, with_ctx=not last, final=last)
        if last:
            out = x2
        else:
            x_first, x_rest, rest_tile0 = x2, x2, 1
    return out
```

```python
import functools
import math

import numpy as np
import jax
import jax.numpy as jnp
from jax import lax
from jax.experimental import pallas as pl
from jax.experimental.pallas import tpu as pltpu

D_MODEL = 1024
SEQ = 2048
DEPTH = 2
CTX_LEN = 256
GRID_W = 64
GRID_ROWS = SEQ // GRID_W
TOK = CTX_LEN + SEQ

NA_HEADS = 4
NA_HEAD_DIM = 64
NA_WIN_R = 8
NA_WIN_C = 16
WA_HEADS = 4
WA_KV_HEADS = 2
WA_HEAD_DIM = 64
WA_WINDOW = 128
WA_BLOCK = 128
GLA_HEADS = 4
GLA_DK = 64
GLA_DV = 128
GLA_GATE_RANK = 16
GLA_GATE_TAU = 16.0
GLA_CHUNK = 64

NA_WIDTH = NA_HEADS * NA_HEAD_DIM
WA_Q_WIDTH = WA_HEADS * WA_HEAD_DIM
WA_KV_WIDTH = WA_KV_HEADS * WA_HEAD_DIM
GLA_QK_WIDTH = GLA_HEADS * GLA_DK
GLA_V_WIDTH = GLA_HEADS * GLA_DV
MIX_WIDTH = NA_WIDTH + WA_Q_WIDTH + GLA_V_WIDTH
IN_WIDTH = 3 * NA_WIDTH + WA_Q_WIDTH + 2 * WA_KV_WIDTH + 2 * GLA_QK_WIDTH + 2 * GLA_V_WIDTH + 2 * GLA_GATE_RANK

FF_DIM = 2816
FF_CHUNK = FF_DIM
ROPE_THETA = 10000.0
EPS = 1e-6

BP = 2
TILE = 256
N_TILES = TOK // TILE
LANES = 128
SUBLANES = 8
NEG = -1e30
VMEM_LIMIT = 56 * 1024 * 1024

F32 = jnp.float32
BF16 = jnp.bfloat16
NT_DIMS = (((1,), (1,)), ((), ()))
TN_DIMS = (((0,), (0,)), ((), ()))

W_NA = 0
W_WAK = 1024
W_WAV = 1152
W_GQ = 1280
W_GK = 1536
W_GV = 1792
W_GO = 2304
W_LR = 2816
X_WAQ = 0
X_LR = 256
X_COLS = 384

WA_HEAD_ORDER = (0, 2, 1, 3)
Q_SCALE = NA_HEAD_DIM ** -0.5


def _cparams(sem):
    return pltpu.CompilerParams(dimension_semantics=sem, vmem_limit_bytes=VMEM_LIMIT)


def _lane_lo(shape, width=LANES):
    lane = lax.broadcasted_iota(jnp.int32, shape, len(shape) - 1)
    return (lane & (width - 1)) < (width // 2)


def _mod_kernel(c_ref, w_ref, b_ref, o_ref):
    c = c_ref[...]
    sc = c * jax.nn.sigmoid(c)
    o_ref[...] = jnp.dot(sc.astype(BF16), w_ref[...].astype(BF16),
                         preferred_element_type=F32) + b_ref[...]


def _mod_call(cc, w_mod, b_mod):
    tn = 1536
    rows = cc.shape[0]
    return pl.pallas_call(
        _mod_kernel,
        grid=(DEPTH, 6 * D_MODEL // tn),
        in_specs=[
            pl.BlockSpec((rows, D_MODEL), lambda l, j: (0, 0)),
            pl.BlockSpec((None, D_MODEL, tn), lambda l, j: (l, 0, j)),
            pl.BlockSpec((None, 1, tn), lambda l, j: (l, 0, j)),
        ],
        out_specs=pl.BlockSpec((None, rows, tn), lambda l, j: (l, 0, j)),
        out_shape=jax.ShapeDtypeStruct((DEPTH, rows, 6 * D_MODEL), F32),
        compiler_params=_cparams(("arbitrary", "arbitrary")),
        name="adaln_mod",
    )(cc, w_mod, b_mod.reshape(DEPTH, 1, 6 * D_MODEL))


N_RO = 2 * NA_WIN_R - 1
N_CO = 2 * NA_WIN_C - 1
N_BIAS = N_RO + 1


def _nabias_kernel(rpb_ref, o_ref):
    h = pl.program_id(0)
    shape = (GRID_W, LANES)
    q = lax.broadcasted_iota(jnp.int32, shape, 0)
    kc = lax.broadcasted_iota(jnp.int32, shape, 1) & (GRID_W - 1)
    d = kc - q + (NA_WIN_C - 1)
    cs = jnp.clip(q - NA_WIN_C // 2, 0, GRID_W - NA_WIN_C)
    inwin = (kc >= cs) & (kc < cs + NA_WIN_C)
    base = h * (N_RO * N_CO)
    o_ref[0] = jnp.full(shape, NEG, F32)
    for ro in range(N_RO):
        acc = jnp.full(shape, NEG, F32)
        for dd in range(N_CO):
            acc = jnp.where(d == dd, rpb_ref[base + ro * N_CO + dd], acc)
        o_ref[1 + ro] = jnp.where(inwin, acc, NEG)


def _nabias_call(rpb):
    return pl.pallas_call(
        _nabias_kernel,
        grid=(NA_HEADS,),
        in_specs=[pl.BlockSpec(memory_space=pltpu.SMEM)],
        out_specs=pl.BlockSpec((None, N_BIAS, GRID_W, LANES), lambda h: (h, 0, 0, 0)),
        out_shape=jax.ShapeDtypeStruct((NA_HEADS, N_BIAS, GRID_W, LANES), F32),
        compiler_params=_cparams(("arbitrary",)),
        name="na_bias",
    )(rpb.reshape(-1))


def _mod_norm(x, g, shift, scale):
    ms = jnp.mean(x * x, axis=-1, keepdims=True)
    return (x * lax.rsqrt(ms + EPS) * g) * (1.0 + scale) + shift


def _pick_tile(first_ref, rest_ref):
    return jnp.where(pl.program_id(1) == 0, first_ref[...], rest_ref[...])


def _rows2d(a):
    return a.reshape(a.shape[0] * a.shape[1], a.shape[2])


def _inproj_kernel(xc_ref, xl_ref, mod_ref, g1_ref, w_ref, wx_ref, gw_ref, gb_ref, cos_ref, sin_ref,
                   na_ref, wa_ref, gqk_ref, gv_ref, go_ref, la_ref):
    mod = mod_ref[...]
    h = _mod_norm(_pick_tile(xc_ref, xl_ref), g1_ref[...], mod[:, :, 0:D_MODEL], mod[:, :, D_MODEL:2 * D_MODEL])
    hb = _rows2d(h).astype(BF16)

    p_main = jnp.dot(hb, w_ref[...], preferred_element_type=F32)
    p_extra = jnp.dot(hb, wx_ref[...], preferred_element_type=F32)

    def main(c0, c1):
        return p_main[:, c0:c1].reshape(BP, TILE, c1 - c0)

    def extra(c0, c1):
        return p_extra[:, c0:c1].reshape(BP, TILE, c1 - c0)

    na_ref[:, :, 0:NA_WIDTH] = (main(W_NA, W_NA + NA_WIDTH) * Q_SCALE).astype(BF16)
    na_ref[:, :, NA_WIDTH:3 * NA_WIDTH] = main(W_NA + NA_WIDTH, W_NA + 3 * NA_WIDTH).astype(BF16)
    cos = cos_ref[...]
    sin = sin_ref[...]

    def rotate_half(a):
        groups = []
        for g in range(a.shape[-1] // LANES):
            blk = _rows2d(a[:, :, g * LANES:(g + 1) * LANES])
            lane = lax.broadcasted_iota(jnp.int32, blk.shape, 1)
            first_half = (lane & (WA_HEAD_DIM // 2)) == 0
            rot = jnp.where(first_half, -pltpu.roll(blk, LANES - WA_HEAD_DIM // 2, 1),
                            pltpu.roll(blk, WA_HEAD_DIM // 2, 1))
            groups.append(rot.reshape(BP, TILE, LANES))
        return jnp.concatenate(groups, axis=-1)

    wa_q = extra(X_WAQ, X_LR)
    wa_k = main(W_WAK, W_WAV)
    wa_ref[:, :, 0:256] = ((wa_q * cos + rotate_half(wa_q) * sin) * Q_SCALE).astype(BF16)
    wa_ref[:, :, 256:384] = (wa_k * cos[:, 0:LANES] + rotate_half(wa_k) * sin[:, 0:LANES]).astype(BF16)
    wa_ref[:, :, 384:512] = main(W_WAV, W_GQ).astype(BF16)
    gqk_ref[:, :, 0:GLA_QK_WIDTH] = main(W_GQ, W_GK) * Q_SCALE
    gqk_ref[:, :, GLA_QK_WIDTH:] = main(W_GK, W_GV)
    gv_ref[...] = main(W_GV, W_GO).astype(BF16)
    go_ref[...] = main(W_GO, W_LR).astype(BF16)
    lr = p_extra[:, X_LR:X_COLS].astype(BF16)
    logit = jnp.dot(lr, gw_ref[...], preferred_element_type=F32) + gb_ref[...]
    log_sig = jnp.minimum(logit, 0.0) - jnp.log(1.0 + jnp.exp(-jnp.abs(logit)))
    la_ref[...] = (log_sig * (1.0 / GLA_GATE_TAU)).reshape(BP, TILE, 2 * GLA_QK_WIDTH)


def _two_source_specs(rest_tile0):
    return (pl.BlockSpec((BP, TILE, D_MODEL), lambda b, i: (b, 0, 0)),
            pl.BlockSpec((BP, TILE, D_MODEL), lambda b, i: (b, jnp.maximum(i - 1, 0) + rest_tile0, 0)))


def _mod_spec(t0):
    return pl.BlockSpec((None, BP, 1, 6 * D_MODEL), lambda b, i: (jnp.minimum(i + t0, 1), b, 0, 0))


def _layer_spec(layer, r, c, **kw):
    return pl.BlockSpec((None, r, c), lambda b, i: (layer, 0, 0), **kw)


def _inproj_call(layer, x_first, x_rest, rest_tile0, mod_rows, g1, w_main, w_extra, gw, gb, cos_t, sin_t):
    B = x_first.shape[0]
    tile = lambda w: pl.BlockSpec((BP, TILE, w), lambda b, i: (b, i, 0))
    spec_c, spec_l = _two_source_specs(rest_tile0)
    return pl.pallas_call(
        _inproj_kernel,
        grid=(B // BP, N_TILES),
        in_specs=[
            spec_c, spec_l,
            _mod_spec(0),
            _layer_spec(layer, 1, D_MODEL),
            _layer_spec(layer, D_MODEL, IN_WIDTH),
            _layer_spec(layer, D_MODEL, X_COLS),
            _layer_spec(layer, LANES, 2 * GLA_QK_WIDTH),
            _layer_spec(layer, 1, 2 * GLA_QK_WIDTH),
            pl.BlockSpec((TILE, 256), lambda b, i: (i, 0)),
            pl.BlockSpec((TILE, 256), lambda b, i: (i, 0)),
        ],
        out_specs=[tile(768), tile(512), tile(512), tile(512), tile(512), tile(512)],
        out_shape=[
            jax.ShapeDtypeStruct((B, TOK, 768), BF16),
            jax.ShapeDtypeStruct((B, TOK, 512), BF16),
            jax.ShapeDtypeStruct((B, TOK, 512), F32),
            jax.ShapeDtypeStruct((B, TOK, 512), BF16),
            jax.ShapeDtypeStruct((B, TOK, 512), BF16),
            jax.ShapeDtypeStruct((B, TOK, 512), F32),
        ],
        compiler_params=_cparams(("arbitrary", "arbitrary")),
        name="in_proj",
    )(x_first, x_rest, mod_rows, g1, w_main, w_extra, gw, gb, cos_t, sin_t)


NA_ROWS_PER_STEP = TILE // GRID_W
NA_STEPS = GRID_ROWS // NA_ROWS_PER_STEP
NA_UNION = NA_WIN_R + NA_ROWS_PER_STEP


def _split_heads_rows(q):
    lo = _lane_lo(q.shape)
    zero = jnp.zeros_like(q)
    return jnp.concatenate([jnp.where(lo, q, zero), jnp.where(lo, zero, q)], axis=0)


def _merge_heads_rows(o, n):
    return jnp.where(_lane_lo((n, LANES)), o[0:n], o[n:2 * n])


def _na_rows(j, slab_ref, bias_ref, o_ref):
    n_union = NA_UNION * GRID_W
    u0 = jnp.clip(j * NA_ROWS_PER_STEP - NA_WIN_R // 2, 0, GRID_ROWS - NA_UNION)
    qrow = pl.multiple_of(CTX_LEN + j * TILE, TILE)
    krow = pl.multiple_of(CTX_LEN + u0 * GRID_W, GRID_W)
    lo = _lane_lo((GRID_W, LANES))
    tile_idx = []
    for a in range(NA_ROWS_PER_STEP):
        r = j * NA_ROWS_PER_STEP + a
        rs = jnp.clip(r - NA_WIN_R // 2, 0, GRID_ROWS - NA_WIN_R)
        idx = []
        for u in range(NA_UNION):
            key_row = u0 + u
            inside = (key_row >= rs) & (key_row < rs + NA_WIN_R)
            idx.append(jnp.where(inside, key_row - r + NA_WIN_R, 0))
        tile_idx.append(idx)
    for p in range(NA_HEADS // 2):
        kc = slab_ref[0:CTX_LEN, 256 + p * LANES:256 + (p + 1) * LANES]
        vc = slab_ref[0:CTX_LEN, 512 + p * LANES:512 + (p + 1) * LANES]
        ku = slab_ref[pl.ds(krow, n_union), 256 + p * LANES:256 + (p + 1) * LANES]
        vu = slab_ref[pl.ds(krow, n_union), 512 + p * LANES:512 + (p + 1) * LANES]
        q_all = slab_ref[pl.ds(qrow, TILE), p * LANES:(p + 1) * LANES]
        q2 = jnp.concatenate([_split_heads_rows(q_all[a * GRID_W:(a + 1) * GRID_W])
                              for a in range(NA_ROWS_PER_STEP)], axis=0)
        bias = jnp.concatenate([
            jnp.concatenate([jnp.where(lo, bias_ref[2 * p + hh, tile_idx[a][2 * jj]],
                                       bias_ref[2 * p + hh, tile_idx[a][2 * jj + 1]])
                             for jj in range(NA_UNION // 2)], axis=1)
            for a in range(NA_ROWS_PER_STEP) for hh in range(2)], axis=0)
        s_u = lax.dot_general(q2, ku, NT_DIMS, preferred_element_type=F32) + bias
        s_c = lax.dot_general(q2, kc, NT_DIMS, preferred_element_type=F32)
        m = jnp.maximum(jnp.max(s_u, axis=-1, keepdims=True), jnp.max(s_c, axis=-1, keepdims=True))
        p_u = jnp.exp(s_u - m)
        p_c = jnp.exp(s_c - m)
        den = jnp.sum(p_u, axis=-1, keepdims=True) + jnp.sum(p_c, axis=-1, keepdims=True)
        o = (jnp.dot(p_u.astype(BF16), vu, preferred_element_type=F32)
             + jnp.dot(p_c.astype(BF16), vc, preferred_element_type=F32)) / den
        out = [_merge_heads_rows(o[a * 2 * GRID_W:(a + 1) * 2 * GRID_W], GRID_W)
               for a in range(NA_ROWS_PER_STEP)]
        o_ref[:, p * LANES:(p + 1) * LANES] = jnp.concatenate(out, axis=0).astype(BF16)


def _na_ctx(slab_ref, o_ref):
    for p in range(NA_HEADS // 2):
        kc = slab_ref[0:CTX_LEN, 256 + p * LANES:256 + (p + 1) * LANES]
        vc = slab_ref[0:CTX_LEN, 512 + p * LANES:512 + (p + 1) * LANES]
        q2 = _split_heads_rows(slab_ref[0:CTX_LEN, p * LANES:(p + 1) * LANES])
        s_c = lax.dot_general(q2, kc, NT_DIMS, preferred_element_type=F32)
        p_c = jnp.exp(s_c - jnp.max(s_c, axis=-1, keepdims=True))
        den = jnp.sum(p_c, axis=-1, keepdims=True)
        o = jnp.dot(p_c.astype(BF16), vc, preferred_element_type=F32) / den
        o_ref[:, p * LANES:(p + 1) * LANES] = _merge_heads_rows(o, CTX_LEN).astype(BF16)


N_WA_BLOCKS = SEQ // WA_BLOCK
WA_PER_STEP = TILE // WA_BLOCK
WA_KCOL = slice(256, 384)
WA_VCOL = slice(384, 512)


def _wa_stack_heads(slab_ref, qrow):
    lo = _lane_lo((WA_BLOCK, LANES))
    qa = slab_ref[pl.ds(qrow, WA_BLOCK), 0:LANES]
    qb = slab_ref[pl.ds(qrow, WA_BLOCK), LANES:2 * LANES]
    zero = jnp.zeros_like(qa)
    return jnp.concatenate([jnp.where(lo, qa, zero), jnp.where(lo, zero, qa),
                            jnp.where(lo, qb, zero), jnp.where(lo, zero, qb)], axis=0)


def _wa_softmax_pv(sink_ref, scores, values, o_ref, out_row):
    rows4 = WA_HEADS * WA_BLOCK
    slot = lax.broadcasted_iota(jnp.int32, (rows4, 1), 0) // WA_BLOCK
    sink = jnp.zeros((rows4, 1), F32)
    for s_i, h in enumerate(WA_HEAD_ORDER):
        sink = jnp.where(slot == s_i, sink_ref[h], sink)
    m = sink
    for s in scores:
        m = jnp.maximum(m, jnp.max(s, axis=-1, keepdims=True))
    den = jnp.exp(sink - m)
    o = None
    for s, v in zip(scores, values):
        e = jnp.exp(s - m)
        den = den + jnp.sum(e, axis=-1, keepdims=True)
        t = jnp.dot(e.astype(BF16), v, preferred_element_type=F32)
        o = t if o is None else o + t
    o = o / den
    b = WA_BLOCK
    lo = _lane_lo((b, LANES))
    rows = slice(out_row, out_row + b)
    o_ref[rows, 0:LANES] = jnp.where(lo, o[0:b], o[b:2 * b]).astype(BF16)
    o_ref[rows, LANES:2 * LANES] = jnp.where(lo, o[2 * b:3 * b], o[3 * b:4 * b]).astype(BF16)


def _wa_scores(q4, k):
    return lax.dot_general(q4, k, NT_DIMS, preferred_element_type=F32)


def _wa_block(n, out_row, sink_ref, slab_ref, o_ref):
    rows4 = WA_HEADS * WA_BLOCK
    qrow = pl.multiple_of(CTX_LEN + n * WA_BLOCK, WA_BLOCK)
    prow = pl.multiple_of(qrow - WA_BLOCK, WA_BLOCK)
    nrow = pl.multiple_of(jnp.minimum(qrow + WA_BLOCK, TOK - WA_BLOCK), WA_BLOCK)
    k_blocks = [slab_ref[pl.ds(s, WA_BLOCK), WA_KCOL] for s in (prow, qrow, nrow)]
    v_blocks = [slab_ref[pl.ds(s, WA_BLOCK), WA_VCOL] for s in (prow, qrow, nrow)]
    i = lax.broadcasted_iota(jnp.int32, (rows4, WA_BLOCK), 0) & (WA_BLOCK - 1)
    jj = lax.broadcasted_iota(jnp.int32, (rows4, WA_BLOCK), 1)
    keep_prev = (jj >= i) & (n > 0)
    keep_next = (jj <= i) & (n < N_WA_BLOCKS - 1)
    q4 = _wa_stack_heads(slab_ref, qrow)
    s_p = jnp.where(keep_prev, _wa_scores(q4, k_blocks[0]), NEG)
    s_m = _wa_scores(q4, k_blocks[1])
    s_n = jnp.where(keep_next, _wa_scores(q4, k_blocks[2]), NEG)
    s_c = _wa_scores(q4, slab_ref[0:CTX_LEN, WA_KCOL])
    _wa_softmax_pv(sink_ref, [s_p, s_m, s_n, s_c], v_blocks + [slab_ref[0:CTX_LEN, WA_VCOL]], o_ref, out_row)


def _wa_ctx(blk, sink_ref, slab_ref, o_ref):
    q4 = _wa_stack_heads(slab_ref, blk * WA_BLOCK)
    s_c = _wa_scores(q4, slab_ref[0:CTX_LEN, WA_KCOL])
    _wa_softmax_pv(sink_ref, [s_c], [slab_ref[0:CTX_LEN, WA_VCOL]], o_ref, blk * WA_BLOCK)


GLA_LEVELS = int(math.log2(GLA_CHUNK))
GLA_ROW_LEVELS = 3
E_ROWS = (1 + GLA_LEVELS - GLA_ROW_LEVELS) * GLA_CHUNK


def _gla_constants():
    C = GLA_CHUNK
    p = np.arange(C)
    r = p[None, :]
    mats = [r <= p[:, None]]
    masks = []
    for lvl in range(GLA_LEVELS):
        m = C >> (lvl + 1)
        pair = p // (2 * m)
        half = (p // m) % 2
        a_end = pair * 2 * m + m - 1
        if lvl >= GLA_ROW_LEVELS:
            mats.append(r <= a_end[:, None])
        masks.append((pair[:, None] == pair[None, :]) & (half[:, None] == 1) & (half[None, :] == 0))
    masks.append(p[:, None] == p[None, :])
    fwd = np.concatenate(mats, axis=0).astype(np.float32)
    bwd = np.concatenate([blk[::-1, ::-1] for blk in mats], axis=0).astype(np.float32)
    msk = np.stack(masks).astype(np.float32)
    lmask = np.stack([msk, msk[:, ::-1, ::-1]])
    lmask = np.concatenate([lmask, lmask], axis=-1)
    return np.stack([fwd, bwd]), lmask


def _gla_tiles(emat_ref, lmask_ref, qkf_ref, laf_ref, vf_ref, qkb_ref, lab_ref, vb_ref,
               of_ref, ob_ref, st_ref):
    C = GLA_CHUNK
    lo = _lane_lo((C, LANES))
    v_lo = _lane_lo((C, 2 * GLA_DV), width=2 * GLA_DV)
    st_rows = lax.broadcasted_iota(jnp.int32, (2 * GLA_DV, LANES), 0) < GLA_DV
    st_keep = st_rows == _lane_lo((2 * GLA_DV, LANES))

    def head_block(kk):
        zero = jnp.zeros_like(kk)
        return jnp.concatenate([jnp.where(lo, kk, zero), jnp.where(lo, zero, kk)], axis=0)

    def chunk(bb, d, p, r0, qk_ref, la_ref, v_ref, o_ref):
        rows = slice(r0, r0 + C)
        g = la_ref[bb, rows, p * LANES:(p + 1) * LANES]
        q = qk_ref[bb, rows, p * LANES:(p + 1) * LANES]
        k = qk_ref[bb, rows, GLA_QK_WIDTH + p * LANES:GLA_QK_WIDTH + (p + 1) * LANES]
        v = v_ref[bb, rows, p * 2 * GLA_DV:(p + 1) * 2 * GLA_DV]
        g_hi = g.astype(BF16)
        g_lo = (g - g_hi.astype(F32)).astype(BF16)
        e2 = jnp.dot(emat_ref[d], jnp.concatenate([g_hi, g_lo], axis=1), preferred_element_type=F32)
        e = e2[:, 0:LANES] + e2[:, LANES:2 * LANES]
        b = e[0:C]
        end_row = C - 1 if d == 0 else 0
        b_end = b[end_row:end_row + 1]
        st = st_ref[bb, d, p]
        o = lax.dot_general((q * jnp.exp(b)).astype(BF16), st.astype(BF16), NT_DIMS,
                            preferred_element_type=F32)
        att = lax.dot_general(q.astype(BF16), head_block(k.astype(BF16)), NT_DIMS,
                              preferred_element_type=F32) * lmask_ref[d, GLA_LEVELS]
        for lvl in range(GLA_LEVELS):
            m = C >> (lvl + 1)
            if lvl < GLA_ROW_LEVELS:
                pieces = []
                for u in range(C // (2 * m)):
                    row = 2 * m * u + (m - 1 if d == 0 else m)
                    pieces.append(jnp.broadcast_to(b[row:row + 1], (2 * m, LANES)))
                rho = pieces[0] if len(pieces) == 1 else jnp.concatenate(pieces, axis=0)
            else:
                rho = e[(lvl - GLA_ROW_LEVELS + 1) * C:(lvl - GLA_ROW_LEVELS + 2) * C]
            x = jnp.exp(-jnp.abs(b - rho))
            att = att + lax.dot_general((q * x).astype(BF16), head_block((k * x).astype(BF16)), NT_DIMS,
                                        preferred_element_type=F32) * lmask_ref[d, lvl]
        vzero = jnp.zeros_like(v)
        v_bd = jnp.concatenate([jnp.where(v_lo, v, vzero), jnp.where(v_lo, vzero, v)], axis=0)
        o = o + jnp.dot(att.astype(BF16), v_bd, preferred_element_type=F32)
        o_ref[bb, rows, p * 2 * GLA_DV:(p + 1) * 2 * GLA_DV] = o.astype(BF16)
        ks = (k * jnp.exp(b_end - b)).astype(BF16)
        upd = lax.dot_general(v, ks, TN_DIMS, preferred_element_type=F32)
        st_ref[bb, d, p] = st * jnp.exp(b_end) + jnp.where(st_keep, upd, 0.0)

    n_chunks = TILE // C
    for step in range(n_chunks):
        for bb in range(BP):
            for p in range(GLA_HEADS // 2):
                chunk(bb, 0, p, step * C, qkf_ref, laf_ref, vf_ref, of_ref)
                chunk(bb, 1, p, (n_chunks - 1 - step) * C, qkb_ref, lab_ref, vb_ref, ob_ref)


def _mixer_kernel(sink_ref, na_ref, wa_ref, bias_ref, emat_ref, lmask_ref, qkf_ref, laf_ref, vf_ref,
                  qkb_ref, lab_ref, vb_ref, ona_ref, owa_ref, of_ref, ob_ref, st_ref, *, with_ctx):
    i = pl.program_id(1)
    gla = functools.partial(_gla_tiles, emat_ref, lmask_ref, qkf_ref, laf_ref, vf_ref, qkb_ref, lab_ref, vb_ref,
                            of_ref, ob_ref, st_ref)

    @pl.when(i == 0)
    def _():
        st_ref[...] = jnp.zeros_like(st_ref)
        gla()
        for bb in range(BP):
            if with_ctx:
                _na_ctx(na_ref.at[bb], ona_ref.at[bb])
                for blk in range(CTX_LEN // WA_BLOCK):
                    _wa_ctx(blk, sink_ref, wa_ref.at[bb], owa_ref.at[bb])
            else:
                ona_ref[bb] = jnp.zeros((TILE, NA_WIDTH), BF16)
                owa_ref[bb] = jnp.zeros((TILE, WA_Q_WIDTH), BF16)

    @pl.when(i > 0)
    def _():
        gla()
        for bb in range(BP):
            _na_rows(i - 1, na_ref.at[bb], bias_ref, ona_ref.at[bb])
            for qb in range(WA_PER_STEP):
                _wa_block((i - 1) * WA_PER_STEP + qb, qb * WA_BLOCK, sink_ref, wa_ref.at[bb], owa_ref.at[bb])


def _mixer_call(na, wa, bias, sink, gqk, la, gv, emat, lmask, with_ctx):
    B = gqk.shape[0]
    fwd = lambda b, i: (b, i, 0)
    bwd = lambda b, i: (b, jnp.where(i == 0, 0, N_TILES - i), 0)
    bwd_la = lambda b, i: (b, jnp.where(i == 0, 0, N_TILES - i), 1)
    slab = lambda w: pl.BlockSpec((BP, TOK, w), lambda b, i: (b, 0, 0))
    return pl.pallas_call(
        functools.partial(_mixer_kernel, with_ctx=with_ctx),
        grid=(B // BP, N_TILES),
        in_specs=[
            pl.BlockSpec(memory_space=pltpu.SMEM),
            slab(768), slab(512),
            pl.BlockSpec((NA_HEADS, N_BIAS, GRID_W, LANES), lambda b, i: (0, 0, 0, 0)),
            pl.BlockSpec((2, E_ROWS, GLA_CHUNK), lambda b, i: (0, 0, 0)),
            pl.BlockSpec((2, GLA_LEVELS + 1, GLA_CHUNK, LANES), lambda b, i: (0, 0, 0, 0)),
            pl.BlockSpec((BP, TILE, 512), fwd),
            pl.BlockSpec((BP, TILE, 256), fwd),
            pl.BlockSpec((BP, TILE, 512), fwd),
            pl.BlockSpec((BP, TILE, 512), bwd),
            pl.BlockSpec((BP, TILE, 256), bwd_la),
            pl.BlockSpec((BP, TILE, 512), bwd),
        ],
        out_specs=[pl.BlockSpec((BP, TILE, NA_WIDTH), fwd), pl.BlockSpec((BP, TILE, WA_Q_WIDTH), fwd),
                   pl.BlockSpec((BP, TILE, 512), fwd), pl.BlockSpec((BP, TILE, 512), bwd)],
        out_shape=[jax.ShapeDtypeStruct((B, TOK, NA_WIDTH), BF16), jax.ShapeDtypeStruct((B, TOK, WA_Q_WIDTH), BF16),
                   jax.ShapeDtypeStruct((B, TOK, 512), BF16), jax.ShapeDtypeStruct((B, TOK, 512), BF16)],
        scratch_shapes=[pltpu.VMEM((BP, 2, GLA_HEADS // 2, 2 * GLA_DV, LANES), F32)],
        compiler_params=_cparams(("arbitrary", "arbitrary")),
        name="token_mixers",
    )(sink, na, wa, bias, emat, lmask, gqk, la, gv, gqk, la, gv)


HALO = SUBLANES


def _ext_rows(prev_ref, tile, next_ref):
    return jnp.concatenate([prev_ref[...].astype(F32), tile.astype(F32), next_ref[...].astype(F32)], axis=1)


def _tail_kernel(xc_ref, xl_ref, xp_ref, xn_ref, mod_ref,
                 na_ref, nap_ref, nan_ref, wa_ref, wap_ref, wan_ref, of_ref, ofp_ref, ofn_ref,
                 ob_ref, obp_ref, obn_ref, go_ref, gop_ref, gon_ref,
                 gn_ref, wo_ref, g2_ref, wv_ref, wg_ref, cw_ref, cb_ref, wd_ref, fg_ref,
                 o_ref, gs_ref, *, t0, two_source, final):
    ti = pl.program_id(1) + t0
    ext = TILE + 2 * HALO
    mod = mod_ref[...]
    x_t = _pick_tile(xc_ref, xl_ref) if two_source else xl_ref[...]
    x_ext = _ext_rows(xp_ref, x_t, xn_ref)

    og = _rows2d(_ext_rows(ofp_ref, of_ref[...], ofn_ref) + _ext_rows(obp_ref, ob_ref[...], obn_ref))
    gn = gn_ref[...]
    parts = []
    for h in range(GLA_HEADS):
        seg = og[:, h * GLA_DV:(h + 1) * GLA_DV]
        ms = jnp.mean(seg * seg, axis=-1, keepdims=True)
        parts.append(seg * lax.rsqrt(ms + EPS) * gn)
    go = _rows2d(_ext_rows(gop_ref, go_ref[...], gon_ref))
    y_g = (jnp.concatenate(parts, axis=1) * (go * jax.nn.sigmoid(go))).astype(BF16)
    y_na = _rows2d(_ext_rows(nap_ref, na_ref[...], nan_ref)).astype(BF16)
    y_wa = _rows2d(_ext_rows(wap_ref, wa_ref[...], wan_ref)).astype(BF16)
    res = (jnp.dot(y_na, wo_ref[0:256, :], preferred_element_type=F32)
           + jnp.dot(y_wa, wo_ref[256:512, :], preferred_element_type=F32)
           + jnp.dot(y_g, wo_ref[512:1024, :], preferred_element_type=F32))
    x1_ext = x_ext + mod[:, :, 2 * D_MODEL:3 * D_MODEL] * res.reshape(BP, ext, D_MODEL)

    shift, scale = mod[:, :, 3 * D_MODEL:4 * D_MODEL], mod[:, :, 4 * D_MODEL:5 * D_MODEL]
    h_all = _mod_norm(x1_ext, g2_ref[...], shift, scale)
    h_ext = _rows2d(h_all).astype(BF16)
    h_b = _rows2d(h_all[:, HALO:HALO + TILE]).astype(BF16)
    seq_start = ti <= 1
    seq_end = (ti == 0) | (ti == N_TILES - 1)
    acc = jnp.zeros((BP * TILE, D_MODEL), F32)
    for c in range(FF_DIM // FF_CHUNK):
        cols = slice(c * FF_CHUNK, (c + 1) * FF_CHUNK)
        g_ext = jnp.dot(h_ext, wg_ref[:, cols], preferred_element_type=F32).reshape(BP, ext, FF_CHUNK)
        g_ext = jnp.concatenate([jnp.where(seq_start, 0.0, g_ext[:, 0:HALO]), g_ext[:, HALO:HALO + TILE],
                                 jnp.where(seq_end, 0.0, g_ext[:, HALO + TILE:])], axis=1)
        mid = slice(HALO, HALO + TILE)
        cw = cw_ref[:, cols]
        gate = (cw[0:1] * pltpu.roll(g_ext, 1, 1)[:, mid] + cw[1:2] * g_ext[:, mid]
                + cw[2:3] * pltpu.roll(g_ext, ext - 1, 1)[:, mid] + cb_ref[:, cols])
        gate = _rows2d(gate)
        val = jnp.dot(h_b, wv_ref[:, cols], preferred_element_type=F32)
        act = 0.5 * gate * (1.0 + lax.erf(gate * (1.0 / math.sqrt(2.0))))
        acc = acc + jnp.dot((act * val).astype(BF16), wd_ref[cols, :], preferred_element_type=F32)
    y = x1_ext[:, HALO:HALO + TILE] + mod[:, :, 5 * D_MODEL:6 * D_MODEL] * acc.reshape(BP, TILE, D_MODEL)
    if final:
        ms = jnp.mean(y * y, axis=-1, keepdims=True)
        y = y * lax.rsqrt(ms + EPS) * fg_ref[...]
    o_ref[...] = y


def _tail_call(layer, x_first, x_rest, rest_tile0, mod_rows, o_na, o_wa, o_f, o_b, go, gn, w_out,
               g2, w_up, cw, cb, wd, fg, with_ctx, final):
    B = x_first.shape[0]
    t0 = 0 if with_ctx else 1
    n_t = N_TILES - t0
    per8 = TILE // HALO

    def tile_and_halos(w):
        last8 = TOK // HALO - 1
        return [pl.BlockSpec((BP, TILE, w), lambda b, i: (b, i + t0, 0)),
                pl.BlockSpec((BP, HALO, w), lambda b, i: (b, jnp.maximum((i + t0) * per8 - 1, 0), 0)),
                pl.BlockSpec((BP, HALO, w), lambda b, i: (b, jnp.minimum((i + t0 + 1) * per8, last8), 0))]

    if with_ctx:
        spec_c, spec_l = _two_source_specs(rest_tile0)
    else:
        spec_c = spec_l = pl.BlockSpec((BP, TILE, D_MODEL), lambda b, i: (b, i + t0, 0))
    rest_last8 = x_rest.shape[1] // HALO - 1
    r0 = rest_tile0 - 1 + t0
    x_prev = pl.BlockSpec((BP, HALO, D_MODEL), lambda b, i: (b, jnp.clip((i + r0) * per8 - 1, 0, rest_last8), 0))
    x_next = pl.BlockSpec((BP, HALO, D_MODEL), lambda b, i: (b, jnp.clip((i + r0 + 1) * per8, 0, rest_last8), 0))
    once = pl.Buffered(1)
    return pl.pallas_call(
        functools.partial(_tail_kernel, t0=t0, two_source=with_ctx, final=final),
        grid=(B // BP, n_t),
        in_specs=[
            spec_c, spec_l, x_prev, x_next, _mod_spec(t0),
            *tile_and_halos(NA_WIDTH), *tile_and_halos(WA_Q_WIDTH),
            *tile_and_halos(512), *tile_and_halos(512), *tile_and_halos(512),
            _layer_spec(layer, 1, GLA_DV),
            _layer_spec(layer, MIX_WIDTH, D_MODEL, pipeline_mode=once),
            _layer_spec(layer, 1, D_MODEL),
            pl.BlockSpec((None, D_MODEL, FF_DIM), lambda b, i: (layer, 0, 0), pipeline_mode=once),
            pl.BlockSpec((None, D_MODEL, FF_DIM), lambda b, i: (layer, 0, 1), pipeline_mode=once),
            _layer_spec(layer, 3, FF_DIM), _layer_spec(layer, 1, FF_DIM),
            _layer_spec(layer, FF_DIM, D_MODEL, pipeline_mode=once),
            pl.BlockSpec((1, D_MODEL), lambda b, i: (0, 0)),
        ],
        out_specs=pl.BlockSpec((BP, TILE, D_MODEL), lambda b, i: (b, i, 0)),
        out_shape=jax.ShapeDtypeStruct((B, n_t * TILE, D_MODEL), F32),
        scratch_shapes=[pltpu.VMEM((BP, TILE + 2 * HALO, FF_CHUNK), F32)],
        compiler_params=_cparams(("arbitrary", "arbitrary")),
        name="layer_tail",
    )(x_first, x_rest, x_rest, x_rest, mod_rows,
      o_na, o_na, o_na, o_wa, o_wa, o_wa, o_f, o_f, o_f, o_b, o_b, o_b, go, go, go,
      gn, w_out, g2, w_up, w_up, cw, cb, wd, fg)


def _rope_tables():
    t = np.arange(SEQ)
    n_freq = WA_HEAD_DIM // 4
    inv = ROPE_THETA ** (-np.arange(n_freq, dtype=np.float32) / n_freq)
    ang = np.concatenate([(t // GRID_W).astype(np.float32)[:, None] * inv,
                          (t % GRID_W).astype(np.float32)[:, None] * inv], axis=-1)
    cos = np.tile(np.cos(ang), (1, 2 * WA_HEADS))
    sin = np.tile(np.sin(ang), (1, 2 * WA_HEADS))
    cos = np.concatenate([np.ones((CTX_LEN, 256), np.float32), cos], axis=0)
    sin = np.concatenate([np.zeros((CTX_LEN, 256), np.float32), sin], axis=0)
    return jnp.asarray(cos, F32), jnp.asarray(sin, F32)


def _stacked_weights(w_in, gate_w, gate_b, w_out):
    order = np.array(WA_HEAD_ORDER)
    waq = w_in[:, :, 3 * NA_WIDTH:3 * NA_WIDTH + WA_Q_WIDTH].reshape(DEPTH, D_MODEL, WA_HEADS, WA_HEAD_DIM)
    waq = waq[:, :, order].reshape(DEPTH, D_MODEL, WA_Q_WIDTH)
    lr_pad = jnp.zeros((DEPTH, D_MODEL, X_COLS - X_LR - 2 * GLA_GATE_RANK), w_in.dtype)
    w_extra = jnp.concatenate([waq, w_in[:, :, W_LR:], lr_pad], axis=-1).astype(BF16)
    zero = jnp.zeros((DEPTH, GLA_GATE_RANK, GLA_QK_WIDTH), F32)
    gw = jnp.concatenate([
        jnp.concatenate([gate_w[:, 0], zero], axis=-1),
        jnp.concatenate([zero, gate_w[:, 1]], axis=-1),
        jnp.zeros((DEPTH, LANES - 2 * GLA_GATE_RANK, 2 * GLA_QK_WIDTH), F32)], axis=1).astype(BF16)
    gb = gate_b.reshape(DEPTH, 1, 2 * GLA_QK_WIDTH)
    wo_wa = w_out[:, NA_WIDTH:NA_WIDTH + WA_Q_WIDTH].reshape(DEPTH, WA_HEADS, WA_HEAD_DIM, D_MODEL)[:, order]
    wo = jnp.concatenate([w_out[:, :NA_WIDTH], wo_wa.reshape(DEPTH, WA_Q_WIDTH, D_MODEL),
                          w_out[:, NA_WIDTH + WA_Q_WIDTH:]], axis=1).astype(BF16)
    return w_in.astype(BF16), w_extra, gw, gb, wo


def kernel(x, c, ctx, c_ctx, w_mod, b_mod, norm1_g, norm2_g, w_in, na_rpb, wa_sink, gla_gate_w,
           gla_gate_b, gla_norm_g, w_out, ffn_w_up, ffn_conv_w, ffn_conv_b, ffn_w_down, final_norm_g):
    B = x.shape[0]
    assert x.shape == (B, SEQ, D_MODEL) and ctx.shape == (B, CTX_LEN, D_MODEL) and B % BP == 0
    n_c = -(-(B + 1) // SUBLANES) * SUBLANES
    cc = jnp.concatenate([c, c_ctx[None], jnp.zeros((n_c - B - 1, D_MODEL), F32)], axis=0)
    mods = _mod_call(cc, w_mod, b_mod)
    cos_t, sin_t = _rope_tables()
    emat, lmask = _gla_constants()
    emat = jnp.asarray(emat, BF16)
    lmask = jnp.asarray(lmask, F32)
    fg = final_norm_g.reshape(1, D_MODEL)
    w_main, w_extra, gw, gb, wo = _stacked_weights(w_in, gla_gate_w, gla_gate_b, w_out)
    w_up = ffn_w_up.astype(BF16)
    w_down = ffn_w_down.astype(BF16)
    g1 = norm1_g.reshape(DEPTH, 1, D_MODEL)
    g2 = norm2_g.reshape(DEPTH, 1, D_MODEL)
    gn = gla_norm_g.reshape(DEPTH, 1, GLA_DV)
    conv_b = ffn_conv_b.reshape(DEPTH, 1, FF_DIM)

    x_first, x_rest, rest_tile0 = ctx, x, 0
    out = None
    for l in range(DEPTH):
        last = l == DEPTH - 1
        mod_rows = jnp.stack([jnp.broadcast_to(mods[l, B], (B, 6 * D_MODEL)), mods[l, :B]],
                             axis=0).reshape(2, B, 1, 6 * D_MODEL)
        na, wa, gqk, gv, go, la = _inproj_call(l, x_first, x_rest, rest_tile0, mod_rows,
                                               g1, w_main, w_extra, gw, gb, cos_t, sin_t)
        bias = _nabias_call(na_rpb[l])
        o_na, o_wa, o_f, o_b = _mixer_call(na, wa, bias, wa_sink[l], gqk, la, gv, emat, lmask, with_ctx=not last)
        x2 = _tail_call(l, x_first, x_rest, rest_tile0, mod_rows, o_na, o_wa, o_f, o_b, go,
                        gn, wo, g2, w_up, ffn_conv_w, conv_b, w_down, fg, with_ctx=not last, final=last)
        if last:
            out = x2
        else:
            x_first, x_rest, rest_tile0 = x2, x2, 1
    return out
```
